```python
import jax, jax.numpy as jnp
from jax import lax
import numpy as np

D_MODEL = 1024
BATCH = 32
SEQ = 256
DEPTH = 2
DEC_BATCH = 4
DEC_SEQ = 2048
PAST_LEN = 512

GRID_W = 64
N_EVEN = (DEPTH + 1) // 2
N_ODD = DEPTH // 2
EPS = 1e-6
H_RET = 4
DK_RET = D_MODEL // 8
DV_RET = 2 * DK_RET
RET_CHUNK = 128
POOL_WINDOWS = (2, 4, 8, 16)
N_POOL_GROUPS = 4
POOL_WIDTH = D_MODEL // 2
POOL_GROUP_DIM = POOL_WIDTH // N_POOL_GROUPS
EVEN_IN = 2 * H_RET * DK_RET + 2 * H_RET * DV_RET + POOL_WIDTH
EVEN_MIX = H_RET * DV_RET + POOL_WIDTH
HEAD_DIM = 128
N_HEADS = D_MODEL // HEAD_DIM
KV_HEADS = 2
Q_BLOCK = 128
ROPE_BASE = 10000.0
ROPE_PAIRS = HEAD_DIM // 4
QKV_OUT = (N_HEADS + 2 * KV_HEADS) * HEAD_DIM
D_FF = -(-8 * D_MODEL // (3 * 256)) * 256

kernel_name = "hybrid_retpool_gqa_diffusion_step"


def rmsnorm(x, gain):
    x32 = x.astype(jnp.float32)
    y = x32 * lax.rsqrt(jnp.mean(x32 * x32, axis=-1, keepdims=True) + EPS)
    return y.astype(x.dtype) * gain


def swiglu(h, w_in, w_out):
    a, b = jnp.split(h @ w_in, 2, axis=-1)
    return (jax.nn.silu(a) * b) @ w_out


def retention_chunked(q, k, v, log_gamma, s0):
    B, T, H, DK = q.shape
    DV = v.shape[-1]
    nc = T // RET_CHUNK
    dt = q.dtype
    n = jnp.arange(RET_CHUNK, dtype=jnp.float32)
    diff = n[:, None] - n[None, :]
    dmask = jnp.where(diff[None] >= 0, jnp.exp(jnp.maximum(diff, 0.0)[None] * log_gamma[:, None, None]), 0.0).astype(dt)
    xi = jnp.exp((n + 1.0)[:, None] * log_gamma[None, :]).astype(dt)
    zeta = jnp.exp((RET_CHUNK - 1.0 - n)[:, None] * log_gamma[None, :]).astype(dt)
    chunk_decay = jnp.exp(RET_CHUNK * log_gamma).astype(dt)

    def to_chunks(a):
        return a.reshape(B, nc, RET_CHUNK, H, a.shape[-1]).transpose(1, 0, 2, 3, 4)

    def step(s, inp):
        qc, kc, vc = inp
        scores = jnp.einsum('bnhd,bmhd->bhnm', qc, kc) * dmask[None]
        inner = jnp.einsum('bhnm,bmhe->bnhe', scores, vc)
        cross = jnp.einsum('bnhd,bhde->bnhe', qc * xi[None, :, :, None], s)
        s_new = s * chunk_decay[None, :, None, None] + jnp.einsum('bmhd,bmhe->bhde', kc * zeta[None, :, :, None], vc)
        return s_new, inner + cross

    s_fin, out = lax.scan(step, s0.astype(dt), (to_chunks(q), to_chunks(k), to_chunks(v)))
    out = out.transpose(1, 0, 2, 3, 4).reshape(B, T, H, DV)
    return out, s_fin


def multiscale_pool(u, pool_w, pool_scale):
    B, T, _ = u.shape
    ug = u.reshape(B, T, N_POOL_GROUPS, POOL_GROUP_DIM)
    csum = jnp.cumsum(ug.astype(jnp.float32), axis=1)
    P = jnp.concatenate([jnp.zeros((B, 1, N_POOL_GROUPS, POOL_GROUP_DIM), jnp.float32), csum], axis=1)
    half = jnp.array([w // 2 for w in POOL_WINDOWS], jnp.int32)
    t = jnp.arange(T, dtype=jnp.int32)
    lo = jnp.clip(t[:, None] - half[None, :], 0, T)
    hi = jnp.clip(t[:, None] + half[None, :], 0, T)
    gidx = jnp.arange(N_POOL_GROUPS, dtype=jnp.int32)[None, :]
    cnt = (hi - lo).astype(jnp.float32)[None, :, :, None]
    mean = (P[:, hi, gidx, :] - P[:, lo, gidx, :]) / cnt
    d = mean.astype(u.dtype) - ug
    out = jnp.einsum('btgc,gcd->btgd', d, pool_w).reshape(B, T, POOL_WIDTH)
    return out * pool_scale


def ret_pool_mixer(h, w_in, decay_logit, gn_gain, pool_w, pool_scale, w_out, s0_f, s0_b):
    B, T, _ = h.shape
    proj = h @ w_in
    sizes = np.cumsum([H_RET * DK_RET, H_RET * DK_RET, H_RET * DV_RET, H_RET * DV_RET])
    q, k, v, g, u = jnp.split(proj, [int(s) for s in sizes], axis=-1)
    q = q.reshape(B, T, H_RET, DK_RET) * (DK_RET ** -0.5)
    k = k.reshape(B, T, H_RET, DK_RET)
    v = v.reshape(B, T, H_RET, DV_RET)
    lg = jax.nn.log_sigmoid(decay_logit.astype(jnp.float32))
    o_f, s_f = retention_chunked(q, k, v, lg[0], s0_f)
    o_b, s_b = retention_chunked(q[:, ::-1], k[:, ::-1], v[:, ::-1], lg[1], s0_b)
    y = (o_f + o_b[:, ::-1]).astype(jnp.float32)
    mu = jnp.mean(y, axis=-1, keepdims=True)
    var = jnp.mean(jnp.square(y - mu), axis=-1, keepdims=True)
    y = ((y - mu) * lax.rsqrt(var + EPS)).astype(h.dtype).reshape(B, T, H_RET * DV_RET) * gn_gain
    ret_out = jax.nn.silu(g) * y
    pool_out = multiscale_pool(u, pool_w, pool_scale)
    out = jnp.concatenate([ret_out, pool_out], axis=-1) @ w_out
    return out, s_f, s_b


def axial_rope(x):
    T = x.shape[1]
    rows = T // GRID_W
    r = jnp.repeat(jnp.arange(rows, dtype=jnp.float32), GRID_W)
    cl = jnp.tile(jnp.arange(GRID_W, dtype=jnp.float32), rows)
    freqs = ROPE_BASE ** (-jnp.arange(ROPE_PAIRS, dtype=jnp.float32) / ROPE_PAIRS)

    def rot(xh, pos):
        ang = pos[:, None] * freqs[None, :]
        cos = jnp.cos(ang).astype(x.dtype)[None, :, None, :]
        sin = jnp.sin(ang).astype(x.dtype)[None, :, None, :]
        x1, x2 = xh[..., :ROPE_PAIRS], xh[..., ROPE_PAIRS:]
        return jnp.concatenate([x1 * cos - x2 * sin, x2 * cos + x1 * sin], axis=-1)

    hd2 = HEAD_DIM // 2
    return jnp.concatenate([rot(x[..., :hd2], r), rot(x[..., hd2:], cl)], axis=-1)


def block_attention(q, k, v):
    B, S, _, D = q.shape
    G = N_HEADS // KV_HEADS
    nb = S // Q_BLOCK
    qb = q.reshape(B, nb, Q_BLOCK, KV_HEADS, G, D).transpose(1, 0, 2, 3, 4, 5)
    scale = HEAD_DIM ** -0.5

    def one_block(qblk):
        s = jnp.einsum('bqkgd,btkd->bkgqt', qblk, k).astype(jnp.float32) * scale
        p = jax.nn.softmax(s, axis=-1).astype(v.dtype)
        return jnp.einsum('bkgqt,btkd->bqkgd', p, v)

    out = lax.map(one_block, qb)
    return out.transpose(1, 0, 2, 3, 4, 5).reshape(B, S, N_HEADS * D)


def attn_project(h, w_qkv, q_gain, k_gain):
    B, T, _ = h.shape
    q, k, v = jnp.split(h @ w_qkv, [N_HEADS * HEAD_DIM, (N_HEADS + KV_HEADS) * HEAD_DIM], axis=-1)
    q = rmsnorm(q.reshape(B, T, N_HEADS, HEAD_DIM), q_gain)
    k = rmsnorm(k.reshape(B, T, KV_HEADS, HEAD_DIM), k_gain)
    v = v.reshape(B, T, KV_HEADS, HEAD_DIM)
    return q, k, v


def setup_inputs(seed: int = 0) -> dict:
    key = jax.random.key(seed)
    ks = jax.random.split(key, 24)
    nrm = lambda k, shape, s: jax.random.normal(k, shape, jnp.float32) * s
    base_gamma = 1.0 - 2.0 ** (-5.0 - jnp.arange(H_RET, dtype=jnp.float32))
    base_logit = jnp.log(base_gamma / (1.0 - base_gamma))
    return {
        "x_prompt": nrm(ks[0], (BATCH, SEQ, D_MODEL), 1.0),
        "x_sample": nrm(ks[1], (DEC_BATCH, DEC_SEQ, D_MODEL), 1.0),
        "state_ret": nrm(ks[2], (DEC_BATCH, N_EVEN, 2, H_RET, DK_RET, DV_RET), 0.5),
        "cache_k": nrm(ks[3], (DEC_BATCH, N_ODD, PAST_LEN, KV_HEADS, HEAD_DIM), 1.0),
        "cache_v": nrm(ks[4], (DEC_BATCH, N_ODD, PAST_LEN, KV_HEADS, HEAD_DIM), 1.0),
        "c": nrm(ks[5], (DEC_BATCH, D_MODEL), 1.0),
        "c_ctx": nrm(ks[6], (D_MODEL,), 1.0),
        "w_ada": nrm(ks[7], (DEPTH, D_MODEL, 6 * D_MODEL), 0.5 * D_MODEL ** -0.5),
        "b_ada": nrm(ks[8], (DEPTH, 6 * D_MODEL), 0.02),
        "norm_gain": 1.0 + nrm(ks[9], (DEPTH, 4, D_MODEL), 0.05),
        "w_ffn_in": nrm(ks[10], (DEPTH, D_MODEL, 2 * D_FF), D_MODEL ** -0.5),
        "w_ffn_out": nrm(ks[11], (DEPTH, D_FF, D_MODEL), D_FF ** -0.5),
        "w_in_even": nrm(ks[12], (N_EVEN, D_MODEL, EVEN_IN), D_MODEL ** -0.5),
        "ret_decay_logit": base_logit[None, None, :] + nrm(ks[13], (N_EVEN, 2, H_RET), 0.1),
        "ret_gn_gain": 1.0 + nrm(ks[14], (N_EVEN, H_RET * DV_RET), 0.05),
        "pool_w": nrm(ks[15], (N_EVEN, N_POOL_GROUPS, POOL_GROUP_DIM, POOL_GROUP_DIM), POOL_GROUP_DIM ** -0.5),
        "pool_scale": 1.0 + nrm(ks[16], (N_EVEN, POOL_WIDTH), 0.1),
        "w_out_even": nrm(ks[17], (N_EVEN, EVEN_MIX, D_MODEL), EVEN_MIX ** -0.5),
        "w_qkv": nrm(ks[18], (N_ODD, D_MODEL, QKV_OUT), D_MODEL ** -0.5),
        "q_norm_gain": 1.0 + nrm(ks[19], (N_ODD, HEAD_DIM), 0.05),
        "k_norm_gain": 1.0 + nrm(ks[20], (N_ODD, HEAD_DIM), 0.05),
        "w_o": nrm(ks[21], (N_ODD, N_HEADS * HEAD_DIM, D_MODEL), (N_HEADS * HEAD_DIM) ** -0.5),
    }


def reference(x_prompt, x_sample, state_ret, cache_k, cache_v, c, c_ctx, w_ada, b_ada, norm_gain,
              w_ffn_in, w_ffn_out, w_in_even, ret_decay_logit, ret_gn_gain, pool_w, pool_scale,
              w_out_even, w_qkv, q_norm_gain, k_norm_gain, w_o):
    xp, xs = x_prompt, x_sample
    new_ret, new_k, new_v = [], [], []
    for i in range(DEPTH):
        j = i // 2
        mod_p = (jax.nn.silu(c_ctx)[None, :] @ w_ada[i] + b_ada[i])[:, None, :]
        mod_s = (jax.nn.silu(c) @ w_ada[i] + b_ada[i])[:, None, :]
        sh1p, sc1p, g1p, sh2p, sc2p, g2p = jnp.split(mod_p, 6, axis=-1)
        sh1s, sc1s, g1s, sh2s, sc2s, g2s = jnp.split(mod_s, 6, axis=-1)
        hp = rmsnorm(xp, norm_gain[i, 0]) * (1.0 + sc1p) + sh1p
        hs = rmsnorm(xs, norm_gain[i, 0]) * (1.0 + sc1s) + sh1s
        if i % 2 == 0:
            zeros = jnp.zeros((xp.shape[0], H_RET, DK_RET, DV_RET), xp.dtype)
            op, sf, sb = ret_pool_mixer(hp, w_in_even[j], ret_decay_logit[j], ret_gn_gain[j], pool_w[j],
                                        pool_scale[j], w_out_even[j], zeros, zeros)
            new_ret.append(jnp.stack([sf, sb], axis=1))
            os_, _, _ = ret_pool_mixer(hs, w_in_even[j], ret_decay_logit[j], ret_gn_gain[j], pool_w[j],
                                       pool_scale[j], w_out_even[j], state_ret[:, j, 0], state_ret[:, j, 1])
        else:
            qp, kp, vp = attn_project(hp, w_qkv[j], q_norm_gain[j], k_norm_gain[j])
            op = block_attention(qp, kp, vp) @ w_o[j]
            new_k.append(kp)
            new_v.append(vp)
            qs, ksl, vs = attn_project(hs, w_qkv[j], q_norm_gain[j], k_norm_gain[j])
            qs, ksl = axial_rope(qs), axial_rope(ksl)
            k_all = jnp.concatenate([cache_k[:, j], ksl], axis=1)
            v_all = jnp.concatenate([cache_v[:, j], vs], axis=1)
            os_ = block_attention(qs, k_all, v_all) @ w_o[j]
        xp = xp + g1p * rmsnorm(op, norm_gain[i, 1])
        xs = xs + g1s * rmsnorm(os_, norm_gain[i, 1])
        fp = swiglu(rmsnorm(xp, norm_gain[i, 2]) * (1.0 + sc2p) + sh2p, w_ffn_in[i], w_ffn_out[i])
        fs = swiglu(rmsnorm(xs, norm_gain[i, 2]) * (1.0 + sc2s) + sh2s, w_ffn_in[i], w_ffn_out[i])
        xp = xp + g2p * rmsnorm(fp, norm_gain[i, 3])
        xs = xs + g2s * rmsnorm(fs, norm_gain[i, 3])
    new_state_ret = jnp.stack(new_ret, axis=1)
    new_cache_k = jnp.stack(new_k, axis=1)
    new_cache_v = jnp.stack(new_v, axis=1)
    return (xp, xs, new_state_ret, new_cache_k, new_cache_v)
```

```python
import functools

import jax
import jax.numpy as jnp
from jax import lax
from jax.experimental import pallas as pl
from jax.experimental.pallas import tpu as pltpu

D_MODEL = 1024
EPS = 1e-6
GRID_W = 64
H_RET = 4
DK_RET = 128
DV_RET = 256
RET_CHUNK = 128
POOL_WINDOWS = (2, 4, 8, 16)
POOL_GROUP_DIM = 128
POOL_WIDTH = 512
POOL_PAD = 8
Q_OFF, K_OFF, V_OFF, G_OFF, U_OFF = 0, 512, 1024, 2048, 3072
EVEN_IN = 3584
HEAD_DIM = 128
N_HEADS = 8
KV_HEADS = 2
GQA = N_HEADS // KV_HEADS
ROPE_BASE = 10000.0
ROPE_PAIRS = 32
QKV_OUT = (N_HEADS + 2 * KV_HEADS) * HEAD_DIM
D_FF = 2816
FF_CHUNK = 1408
MOD_ROWS = 8
MOD_PARTS = 6

BF = jnp.bfloat16
F32 = jnp.float32
VMEM_LIMIT_BYTES = 52 * 1024 * 1024
TOKEN_TILE = 256


def _params(*sem):
    return pltpu.CompilerParams(dimension_semantics=sem, vmem_limit_bytes=VMEM_LIMIT_BYTES)


def _resident(shape):
    nd = len(shape)
    return pl.BlockSpec(shape, lambda *_: (0,) * nd, pipeline_mode=pl.Buffered(1))


def _rms(x):
    return x * lax.rsqrt(jnp.mean(x * x, axis=-1, keepdims=True) + EPS)


def _modnorm(x, gain, sc, sh):
    return (_rms(x) * gain) * (1.0 + sc) + sh


def _silu(x):
    return x * jax.nn.sigmoid(x)


def _dot(a, b):
    return jnp.dot(a, b, preferred_element_type=F32)


def _dot_nt(a, b):
    return lax.dot_general(a, b, (((1,), (1,)), ((), ())), preferred_element_type=F32)


def _dot_tn(a, b):
    return lax.dot_general(a, b, (((0,), (0,)), ((), ())), preferred_element_type=F32)


def _mod_kernel(c_ref, w_ref, b_ref, o_ref):
    s = _silu(c_ref[...]).astype(BF)
    o_ref[0] = _dot(s, w_ref[0].astype(BF)) + b_ref[0]


def _modulation(cvec, w_ada, b_ada):
    depth = w_ada.shape[0]
    n = w_ada.shape[2]
    tn = 1536
    return pl.pallas_call(
        _mod_kernel,
        out_shape=jax.ShapeDtypeStruct((depth, MOD_ROWS, n), F32),
        grid=(depth, n // tn),
        in_specs=[
            pl.BlockSpec((MOD_ROWS, D_MODEL), lambda l, j: (0, 0)),
            pl.BlockSpec((1, D_MODEL, tn), lambda l, j: (l, 0, j)),
            pl.BlockSpec((1, 1, tn), lambda l, j: (l, 0, j)),
        ],
        out_specs=pl.BlockSpec((1, MOD_ROWS, tn), lambda l, j: (l, 0, j)),
        compiler_params=_params("parallel", "parallel"),
        name="adaln_mod",
    )(cvec, w_ada, b_ada.reshape(depth, 1, n))


class _Mod:
    def __init__(self, mod_flat, layer, tokens_per_row):
        self.arr = mod_flat
        self.layer = layer
        self.tokens_per_row = tokens_per_row

    def spec(self, part, tile):
        def index(i, *_):
            row = 0 if self.tokens_per_row is None else 1 + (i * tile) // self.tokens_per_row
            return ((self.layer * MOD_ROWS + row) * MOD_PARTS + part, 0, 0)

        return pl.BlockSpec((1, 1, D_MODEL), index)


def _row_spec(tile, width):
    return pl.BlockSpec((tile, width), lambda i: (i, 0))


def _vec_spec(width):
    return pl.BlockSpec((1, width), lambda i: (0, 0))


def _in_proj_kernel(x_ref, gain_ref, sc_ref, sh_ref, w_ref, o_ref):
    h = _modnorm(x_ref[...], gain_ref[...], sc_ref[0], sh_ref[0]).astype(BF)
    o_ref[...] = _dot(h, w_ref[...])


def _in_proj(x, gain, mod, w):
    n_tok = x.shape[0]
    n = w.shape[1]
    tm = TOKEN_TILE
    return pl.pallas_call(
        _in_proj_kernel,
        out_shape=jax.ShapeDtypeStruct((n_tok, n), F32),
        grid=(n_tok // tm,),
        in_specs=[_row_spec(tm, D_MODEL), _vec_spec(D_MODEL), mod.spec(1, tm), mod.spec(0, tm), _resident(w.shape)],
        out_specs=_row_spec(tm, n),
        compiler_params=_params("parallel"),
        name="in_proj",
    )(x, gain, mod.arr, mod.arr, w)


def _ret_kernel(lg_ref, q_ref, k_ref, v_ref, g_ref, gn_ref, *rest, nc, has_s0, emit_state):
    rest = list(rest)
    s0f_ref = rest.pop(0) if has_s0 else None
    s0b_ref = rest.pop(0) if has_s0 else None
    y_ref = rest.pop(0)
    sfin_ref = rest.pop(0) if emit_state else None
    acc_ref, s_ref = rest

    c_len = RET_CHUNK
    head = pl.program_id(1)
    lgf = lg_ref[0, head]
    lgb = lg_ref[1, head]
    n_col = lax.broadcasted_iota(jnp.int32, (c_len, 1), 0).astype(F32)
    n_row = lax.broadcasted_iota(jnp.int32, (1, c_len), 1).astype(F32)
    diff = n_col - n_row
    dmask_f = jnp.where(diff >= 0, jnp.exp(jnp.maximum(diff, 0.0) * lgf), 0.0)
    dmask_b = jnp.where(diff <= 0, jnp.exp(jnp.maximum(-diff, 0.0) * lgb), 0.0)
    xi_f = jnp.exp((n_col + 1.0) * lgf)
    xi_b = jnp.exp((c_len - n_col) * lgb)
    zeta_f = jnp.exp((c_len - 1.0 - n_col) * lgf)
    zeta_b = jnp.exp(n_col * lgb)
    ones_row = jnp.ones((1, DV_RET), F32)
    cd_f = jnp.exp(ones_row * (c_len * lgf))
    cd_b = jnp.exp(ones_row * (c_len * lgb))
    qscale = DK_RET ** -0.5

    def chunk_terms(c, dmask, xi, zeta, cd):
        rows = pl.ds(pl.multiple_of(c * c_len, c_len), c_len)
        qc = q_ref[0, rows, :] * qscale
        kc = k_ref[0, rows, :]
        vc = v_ref[0, rows, :].astype(BF)
        scores = _dot_nt(qc.astype(BF), kc.astype(BF)) * dmask
        inner = _dot(scores.astype(BF), vc)
        cross = _dot((qc * xi).astype(BF), s_ref[...].astype(BF))
        s_ref[...] = s_ref[...] * cd + _dot_tn((kc * zeta).astype(BF), vc)
        return rows, inner + cross

    def init_state(s0_ref):
        if has_s0:
            s_ref[...] = s0_ref[0, 0, 0, 0]
        else:
            s_ref[...] = jnp.zeros_like(s_ref)

    init_state(s0f_ref)

    def fwd(c, carry):
        rows, o = chunk_terms(c, dmask_f, xi_f, zeta_f, cd_f)
        acc_ref[rows, :] = o
        return carry

    lax.fori_loop(0, nc, fwd, 0)
    if emit_state:
        sfin_ref[0, 0, 0] = s_ref[...]

    init_state(s0b_ref)
    gn = gn_ref[...]

    def bwd(i, carry):
        rows, o = chunk_terms(nc - 1 - i, dmask_b, xi_b, zeta_b, cd_b)
        y = acc_ref[rows, :] + o
        mu = jnp.mean(y, axis=-1, keepdims=True)
        yc = y - mu
        var = jnp.mean(yc * yc, axis=-1, keepdims=True)
        yn = (yc * lax.rsqrt(var + EPS)) * gn
        y_ref[0, rows, :] = _silu(g_ref[0, rows, :]) * yn
        return carry

    lax.fori_loop(0, nc, bwd, 0)
    if emit_state:
        sfin_ref[0, 1, 0] = s_ref[...]


def _retention(proj, lg, gn_gain, state):
    b, t, _ = proj.shape
    nc = t // RET_CHUNK
    has_s0 = state is not None
    emit_state = not has_s0
    qk_blk = lambda off: pl.BlockSpec((1, t, DK_RET), lambda i, h, lg: (i, 0, off // DK_RET + h))
    v_blk = lambda off: pl.BlockSpec((1, t, DV_RET), lambda i, h, lg: (i, 0, off // DV_RET + h))
    in_specs = [qk_blk(Q_OFF), qk_blk(K_OFF), v_blk(V_OFF), v_blk(G_OFF),
                pl.BlockSpec((1, DV_RET), lambda i, h, lg: (0, h))]
    args = [proj, proj, proj, proj, gn_gain]
    if has_s0:
        for direction in (0, 1):
            in_specs.append(pl.BlockSpec((1, 1, 1, 1, DK_RET, DV_RET),
                                         lambda i, h, lg, d=direction: (i, 0, d, h, 0, 0)))
            args.append(state)
    out_shape = [jax.ShapeDtypeStruct((b, t, H_RET * DV_RET), F32)]
    out_specs = [pl.BlockSpec((1, t, DV_RET), lambda i, h, lg: (i, 0, h))]
    if emit_state:
        out_shape.append(jax.ShapeDtypeStruct((b, 2, H_RET, DK_RET, DV_RET), F32))
        out_specs.append(pl.BlockSpec((1, 2, 1, DK_RET, DV_RET), lambda i, h, lg: (i, 0, h, 0, 0)))
    outs = pl.pallas_call(
        functools.partial(_ret_kernel, nc=nc, has_s0=has_s0, emit_state=emit_state),
        out_shape=out_shape,
        grid_spec=pltpu.PrefetchScalarGridSpec(
            num_scalar_prefetch=1,
            grid=(b, H_RET),
            in_specs=in_specs,
            out_specs=out_specs,
            scratch_shapes=[pltpu.VMEM((t, DV_RET), F32), pltpu.VMEM((DK_RET, DV_RET), F32)],
        ),
        compiler_params=_params("parallel", "parallel"),
        name="retention",
    )(lg, *args)
    return outs if emit_state else (outs[0], None)


def _pool_kernel(u_ref, pw_ref, ps_ref, o_ref, pad_ref, *, t):
    zeros = jnp.zeros((POOL_PAD, POOL_WIDTH), F32)
    pad_ref[0:POOL_PAD, :] = zeros
    pad_ref[POOL_PAD + t:2 * POOL_PAD + t, :] = zeros
    pad_ref[POOL_PAD:POOL_PAD + t, :] = u_ref[0]
    pos = lax.broadcasted_iota(jnp.int32, (t, 1), 0)
    for g, window in enumerate(POOL_WINDOWS):
        half = window // 2
        lanes = slice(g * POOL_GROUP_DIM, (g + 1) * POOL_GROUP_DIM)
        acc = jnp.zeros((t, POOL_GROUP_DIM), F32)
        for j in range(-half, half):
            acc = acc + pad_ref[POOL_PAD + j:POOL_PAD + j + t, lanes]
        cnt = (jnp.minimum(pos + half, t) - jnp.maximum(pos - half, 0)).astype(F32)
        d = acc / cnt - u_ref[0, :, lanes]
        o_ref[0, :, lanes] = _dot(d.astype(BF), pw_ref[g]) * ps_ref[:, lanes]


def _pool(proj, pool_w, pool_scale):
    b, t, _ = proj.shape
    return pl.pallas_call(
        functools.partial(_pool_kernel, t=t),
        out_shape=jax.ShapeDtypeStruct((b, t, POOL_WIDTH), F32),
        grid=(b,),
        in_specs=[
            pl.BlockSpec((1, t, POOL_WIDTH), lambda i: (i, 0, U_OFF // POOL_WIDTH)),
            pl.BlockSpec(pool_w.shape, lambda i: (0, 0, 0)),
            pl.BlockSpec((1, POOL_WIDTH), lambda i: (0, 0)),
        ],
        out_specs=pl.BlockSpec((1, t, POOL_WIDTH), lambda i: (i, 0, 0)),
        scratch_shapes=[pltpu.VMEM((t + 2 * POOL_PAD, POOL_WIDTH), F32)],
        compiler_params=_params("parallel"),
        name="pool",
    )(proj, pool_w, pool_scale)


def _out_proj_kernel(*refs, n_in):
    x_ref = refs[0]
    a_refs = refs[1:1 + n_in]
    w_refs = refs[1 + n_in:1 + 2 * n_in]
    gain_ref, gate_ref, o_ref = refs[1 + 2 * n_in:]
    o = _dot(a_refs[0][...].astype(BF), w_refs[0][...])
    for a_ref, w_ref in zip(a_refs[1:], w_refs[1:]):
        o = o + _dot(a_ref[...].astype(BF), w_ref[...])
    o_ref[...] = x_ref[...] + gate_ref[0] * (_rms(o) * gain_ref[...])


def _out_proj(x, acts, weights, gain, mod):
    n_tok = x.shape[0]
    tm = TOKEN_TILE
    n_in = len(acts)
    in_specs = [_row_spec(tm, D_MODEL)]
    in_specs += [_row_spec(tm, a.shape[1]) for a in acts]
    in_specs += [_resident(w.shape) for w in weights]
    in_specs += [_vec_spec(D_MODEL), mod.spec(2, tm)]
    return pl.pallas_call(
        functools.partial(_out_proj_kernel, n_in=n_in),
        out_shape=jax.ShapeDtypeStruct((n_tok, D_MODEL), F32),
        grid=(n_tok // tm,),
        in_specs=in_specs,
        out_specs=_row_spec(tm, D_MODEL),
        compiler_params=_params("parallel"),
        name="out_proj",
    )(x, *acts, *weights, gain, mod.arr)


def _ffn_kernel(x_ref, gain_in_ref, sc_ref, sh_ref, gate_ref, gain_out_ref, win_ref, wout_ref, o_ref):
    x = x_ref[...]
    h = _modnorm(x, gain_in_ref[...], sc_ref[0], sh_ref[0]).astype(BF)
    f = None
    for j in range(D_FF // FF_CHUNK):
        a = _dot(h, win_ref[:, j * FF_CHUNK:(j + 1) * FF_CHUNK])
        b = _dot(h, win_ref[:, D_FF + j * FF_CHUNK:D_FF + (j + 1) * FF_CHUNK])
        part = _dot((_silu(a) * b).astype(BF), wout_ref[j * FF_CHUNK:(j + 1) * FF_CHUNK, :])
        f = part if f is None else f + part
    o_ref[...] = x + gate_ref[0] * (_rms(f) * gain_out_ref[...])


def _ffn(x, gain_in, gain_out, mod, w_in, w_out):
    n_tok = x.shape[0]
    tm = TOKEN_TILE
    return pl.pallas_call(
        _ffn_kernel,
        out_shape=jax.ShapeDtypeStruct((n_tok, D_MODEL), F32),
        grid=(n_tok // tm,),
        in_specs=[_row_spec(tm, D_MODEL), _vec_spec(D_MODEL), mod.spec(4, tm), mod.spec(3, tm), mod.spec(5, tm),
                  _vec_spec(D_MODEL), _resident(w_in.shape), _resident(w_out.shape)],
        out_specs=_row_spec(tm, D_MODEL),
        compiler_params=_params("parallel"),
        name="ffn",
    )(x, gain_in, mod.arr, mod.arr, mod.arr, gain_out, w_in, w_out)


def _rope(x, cs, sn):
    lane = lax.broadcasted_iota(jnp.int32, x.shape, 1)
    first = (lane % (2 * ROPE_PAIRS)) < ROPE_PAIRS
    partner = jnp.where(first, pltpu.roll(x, HEAD_DIM - ROPE_PAIRS, 1), pltpu.roll(x, ROPE_PAIRS, 1))
    return x * cs + partner * sn


def _qkv_kernel(x_ref, gain_ref, sc_ref, sh_ref, w_ref, qg_ref, kg_ref, *rest, rope):
    if rope:
        cs_ref, sn_ref, q_ref, k_ref, v_ref = rest
    else:
        q_ref, k_ref, v_ref = rest
    h = _modnorm(x_ref[...], gain_ref[...], sc_ref[0], sh_ref[0]).astype(BF)
    qkv = _dot(h, w_ref[...])

    def head(idx, g_ref):
        xh = _rms(qkv[:, idx * HEAD_DIM:(idx + 1) * HEAD_DIM]) * g_ref[...]
        return _rope(xh, cs_ref[...], sn_ref[...]) if rope else xh

    for i in range(N_HEADS):
        q_ref[:, i * HEAD_DIM:(i + 1) * HEAD_DIM] = head(i, qg_ref)
    for i in range(KV_HEADS):
        k_ref[:, i * HEAD_DIM:(i + 1) * HEAD_DIM] = head(N_HEADS + i, kg_ref)
    v_ref[...] = qkv[:, (N_HEADS + KV_HEADS) * HEAD_DIM:]


def _qkv(x, gain, mod, w, q_gain, k_gain, rope_tables):
    n_tok = x.shape[0]
    tm = TOKEN_TILE
    rope = rope_tables is not None
    in_specs = [_row_spec(tm, D_MODEL), _vec_spec(D_MODEL), mod.spec(1, tm), mod.spec(0, tm), _resident(w.shape),
                _vec_spec(HEAD_DIM), _vec_spec(HEAD_DIM)]
    args = [x, gain, mod.arr, mod.arr, w, q_gain, k_gain]
    if rope:
        tiles_per_seq = rope_tables[0].shape[0] // tm
        in_specs += [pl.BlockSpec((tm, HEAD_DIM), lambda i: (i % tiles_per_seq, 0))] * 2
        args += list(rope_tables)
    kv_w = KV_HEADS * HEAD_DIM
    return pl.pallas_call(
        functools.partial(_qkv_kernel, rope=rope),
        out_shape=[jax.ShapeDtypeStruct((n_tok, D_MODEL), F32), jax.ShapeDtypeStruct((n_tok, kv_w), F32),
                   jax.ShapeDtypeStruct((n_tok, kv_w), F32)],
        grid=(n_tok // tm,),
        in_specs=in_specs,
        out_specs=[_row_spec(tm, D_MODEL), _row_spec(tm, kv_w), _row_spec(tm, kv_w)],
        compiler_params=_params("parallel"),
        name="qkv_proj",
    )(*args)


def _rope_tables(t):
    pos = jnp.arange(t, dtype=jnp.int32)
    row = (pos // GRID_W).astype(F32)
    col = (pos % GRID_W).astype(F32)
    freqs = ROPE_BASE ** (-jnp.arange(ROPE_PAIRS, dtype=F32) / ROPE_PAIRS)
    ang_r = row[:, None] * freqs[None, :]
    ang_c = col[:, None] * freqs[None, :]
    cs = jnp.concatenate([jnp.cos(ang_r)] * 2 + [jnp.cos(ang_c)] * 2, axis=-1)
    sn = jnp.concatenate([-jnp.sin(ang_r), jnp.sin(ang_r), -jnp.sin(ang_c), jnp.sin(ang_c)], axis=-1)
    return cs, sn


def _attn_kernel(q_ref, k_ref, v_ref, o_ref):
    k = k_ref[0].astype(BF)
    v = v_ref[0].astype(BF)
    scale = HEAD_DIM ** -0.5
    for g in range(GQA):
        lanes = slice(g * HEAD_DIM, (g + 1) * HEAD_DIM)
        s = _dot_nt(q_ref[0, :, lanes].astype(BF), k) * scale
        e = jnp.exp(s - jnp.max(s, axis=-1, keepdims=True))
        denom = jnp.sum(e, axis=-1, keepdims=True)
        o_ref[0, :, lanes] = _dot(e.astype(BF), v) / denom


def _attention(q, k, v):
    b, t, _ = q.shape
    tk = k.shape[1]
    tq = 256
    gw = GQA * HEAD_DIM
    return pl.pallas_call(
        _attn_kernel,
        out_shape=jax.ShapeDtypeStruct(q.shape, F32),
        grid=(b, KV_HEADS, t // tq),
        in_specs=[
            pl.BlockSpec((1, tq, gw), lambda i, h, j: (i, j, h)),
            pl.BlockSpec((1, tk, HEAD_DIM), lambda i, h, j: (i, 0, h)),
            pl.BlockSpec((1, tk, HEAD_DIM), lambda i, h, j: (i, 0, h)),
        ],
        out_specs=pl.BlockSpec((1, tq, gw), lambda i, h, j: (i, j, h)),
        compiler_params=_params("parallel", "parallel", "parallel"),
        name="attention",
    )(q, k, v)


def _even_layer(x3, mod, norm_gain, w_in, lg, gn_gain, pool_w, pool_scale, w_out_ret, w_out_pool, state):
    b, t, _ = x3.shape
    x = x3.reshape(b * t, D_MODEL)
    proj = _in_proj(x, norm_gain[0:1], mod, w_in).reshape(b, t, EVEN_IN)
    ret, s_fin = _retention(proj, lg, gn_gain, state)
    pooled = _pool(proj, pool_w, pool_scale)
    x = _out_proj(x, [ret.reshape(b * t, -1), pooled.reshape(b * t, -1)], [w_out_ret, w_out_pool],
                  norm_gain[1:2], mod)
    return x, s_fin


def _odd_layer(x, b, mod, norm_gain, w_qkv, q_gain, k_gain, w_o, cache_k, cache_v, rope_tables):
    t = x.shape[0] // b
    q, k, v = _qkv(x, norm_gain[0:1], mod, w_qkv, q_gain, k_gain, rope_tables)
    k3 = k.reshape(b, t, -1)
    v3 = v.reshape(b, t, -1)
    if cache_k is None:
        k_all, v_all = k3, v3
    else:
        k_all = jnp.concatenate([cache_k.reshape(b, -1, KV_HEADS * HEAD_DIM), k3], axis=1)
        v_all = jnp.concatenate([cache_v.reshape(b, -1, KV_HEADS * HEAD_DIM), v3], axis=1)
    att = _attention(q.reshape(b, t, -1), k_all, v_all).reshape(b * t, -1)
    x = _out_proj(x, [att], [w_o], norm_gain[1:2], mod)
    return x, k3, v3


def kernel(x_prompt, x_sample, state_ret, cache_k, cache_v, c, c_ctx, w_ada, b_ada, norm_gain, w_ffn_in, w_ffn_out,
           w_in_even, ret_decay_logit, ret_gn_gain, pool_w, pool_scale, w_out_even, w_qkv, q_norm_gain, k_norm_gain,
           w_o):
    bp, tp, _ = x_prompt.shape
    bs, ts, _ = x_sample.shape
    depth = w_ada.shape[0]
    assert bs + 1 <= MOD_ROWS and depth == 2

    cvec = jnp.zeros((MOD_ROWS, D_MODEL), F32).at[0].set(c_ctx).at[1:1 + bs].set(c)
    mod_flat = _modulation(cvec, w_ada, b_ada).reshape(depth * MOD_ROWS * MOD_PARTS, 1, D_MODEL)
    bf = lambda w: w.astype(BF)

    mod_p = _Mod(mod_flat, 0, None)
    mod_s = _Mod(mod_flat, 0, ts)
    lg = jax.nn.log_sigmoid(ret_decay_logit[0].astype(F32))
    w_in = bf(w_in_even[0])
    w_out_ret = bf(w_out_even[0, :H_RET * DV_RET])
    w_out_pool = bf(w_out_even[0, H_RET * DV_RET:])
    even = functools.partial(_even_layer, norm_gain=norm_gain[0], w_in=w_in, lg=lg, gn_gain=ret_gn_gain[0:1],
                             pool_w=bf(pool_w[0]), pool_scale=pool_scale[0:1], w_out_ret=w_out_ret,
                             w_out_pool=w_out_pool)
    xp, s_fin = even(x_prompt, mod_p, state=None)
    xs, _ = even(x_sample, mod_s, state=state_ret)
    ffn0 = functools.partial(_ffn, gain_in=norm_gain[0, 2:3], gain_out=norm_gain[0, 3:4], w_in=bf(w_ffn_in[0]),
                             w_out=bf(w_ffn_out[0]))
    xp = ffn0(xp, mod=mod_p)
    xs = ffn0(xs, mod=mod_s)

    mod_p = _Mod(mod_flat, 1, None)
    mod_s = _Mod(mod_flat, 1, ts)
    odd = functools.partial(_odd_layer, norm_gain=norm_gain[1], w_qkv=bf(w_qkv[0]), q_gain=q_norm_gain[0:1],
                            k_gain=k_norm_gain[0:1], w_o=bf(w_o[0]))
    xp, kp, vp = odd(xp, bp, mod_p, cache_k=None, cache_v=None, rope_tables=None)
    xs, _, _ = odd(xs, bs, mod_s, cache_k=cache_k[:, 0], cache_v=cache_v[:, 0], rope_tables=_rope_tables(ts))
    ffn1 = functools.partial(_ffn, gain_in=norm_gain[1, 2:3], gain_out=norm_gain[1, 3:4], w_in=bf(w_ffn_in[1]),
                             w_out=bf(w_ffn_out[1]))
    xp = ffn1(xp, mod=mod_p)
    xs = ffn1(xs, mod=mod_s)

    new_state_ret = s_fin.reshape(bp, 1, 2, H_RET, DK_RET, DV_RET)
    new_cache_k = kp.reshape(bp, 1, tp, KV_HEADS, HEAD_DIM)
    new_cache_v = vp.reshape(bp, 1, tp, KV_HEADS, HEAD_DIM)
    return (xp.reshape(bp, tp, D_MODEL), xs.reshape(bs, ts, D_MODEL), new_state_ret, new_cache_k, new_cache_v)
```

```python
import functools

import jax
import jax.numpy as jnp
from jax import lax
from jax.experimental import pallas as pl
from jax.experimental.pallas import tpu as pltpu

D_MODEL = 1024
EPS = 1e-6
GRID_W = 64
H_RET = 4
DK_RET = 128
DV_RET = 256
POOL_WINDOWS = (2, 4, 8, 16)
POOL_GROUP_DIM = 128
POOL_WIDTH = 512
POOL_PAD = 8
Q_OFF, K_OFF, V_OFF, G_OFF, U_OFF = 0, 512, 1024, 2048, 3072
EVEN_IN = 3584
HEAD_DIM = 128
N_HEADS = 8
KV_HEADS = 2
GQA = N_HEADS // KV_HEADS
ROPE_BASE = 10000.0
ROPE_PAIRS = 32
QKV_OUT = (N_HEADS + 2 * KV_HEADS) * HEAD_DIM
D_FF = 2816
FF_CHUNKS = ((0, 1536), (1536, 1280))
MOD_ROWS = 8
MOD_PARTS = 6

BF = jnp.bfloat16
F32 = jnp.float32
VMEM_LIMIT_BYTES = 52 * 1024 * 1024
TOKEN_TILE = 256


def _params(*sem):
    return pltpu.CompilerParams(dimension_semantics=sem, vmem_limit_bytes=VMEM_LIMIT_BYTES)


def _resident(shape):
    nd = len(shape)
    return pl.BlockSpec(shape, lambda *_: (0,) * nd, pipeline_mode=pl.Buffered(1))


def _rms(x):
    return x * lax.rsqrt(jnp.mean(x * x, axis=-1, keepdims=True) + EPS)


def _modnorm(x, gain, sc, sh):
    return (_rms(x) * gain) * (1.0 + sc) + sh


def _silu(x):
    return x * jax.nn.sigmoid(x)


def _dot(a, b):
    return jnp.dot(a, b, preferred_element_type=F32)


def _dot_nt(a, b):
    return lax.dot_general(a, b, (((1,), (1,)), ((), ())), preferred_element_type=F32)


def _dot_tn(a, b):
    return lax.dot_general(a, b, (((0,), (0,)), ((), ())), preferred_element_type=F32)


def _mod_kernel(c_ref, w_ref, b_ref, o_ref):
    s = _silu(c_ref[...]).astype(BF)
    o_ref[0] = _dot(s, w_ref[0].astype(BF)) + b_ref[0]


def _modulation(cvec, w_ada, b_ada):
    depth = w_ada.shape[0]
    n = w_ada.shape[2]
    tn = 1536
    return pl.pallas_call(
        _mod_kernel,
        out_shape=jax.ShapeDtypeStruct((depth, MOD_ROWS, n), F32),
        grid=(depth, n // tn),
        in_specs=[
            pl.BlockSpec((MOD_ROWS, D_MODEL), lambda l, j: (0, 0)),
            pl.BlockSpec((1, D_MODEL, tn), lambda l, j: (l, 0, j)),
            pl.BlockSpec((1, 1, tn), lambda l, j: (l, 0, j)),
        ],
        out_specs=pl.BlockSpec((1, MOD_ROWS, tn), lambda l, j: (l, 0, j)),
        compiler_params=_params("parallel", "parallel"),
        name="adaln_mod",
    )(cvec, w_ada, b_ada.reshape(depth, 1, n))


class _Mod:
    def __init__(self, mod_flat, layer, tokens_per_row):
        self.arr = mod_flat
        self.layer = layer
        self.tokens_per_row = tokens_per_row

    def _block(self, row, part):
        return ((self.layer * MOD_ROWS + row) * MOD_PARTS + part, 0, 0)

    def spec(self, part, tile):
        def index(i, *_):
            return self._block(0 if self.tokens_per_row is None else 1 + (i * tile) // self.tokens_per_row, part)

        return pl.BlockSpec((1, 1, D_MODEL), index)

    def spec_batch(self, part):
        def index(b, *_):
            return self._block(0 if self.tokens_per_row is None else 1 + b, part)

        return pl.BlockSpec((1, 1, D_MODEL), index)


def _row_spec(tile, width):
    return pl.BlockSpec((tile, width), lambda i: (i, 0))


def _vec_spec(width):
    return pl.BlockSpec((1, width), lambda *_: (0, 0))


def _positions(n):
    return lax.broadcasted_iota(jnp.int32, (n, 1), 0).astype(F32)


def _in_proj_kernel(lg_ref, x_ref, gain_ref, sc_ref, sh_ref, w_ref, *rest, nt, has_s0, emit_state):
    rest = list(rest)
    s0_ref = rest.pop(0) if has_s0 else None
    proj_ref = rest.pop(0)
    sstart_ref = rest.pop(0)
    sfin_ref = rest.pop(0) if emit_state else None
    (s_ref,) = rest
    j = pl.program_id(1)
    c_len = TOKEN_TILE

    @pl.when(j == 0)
    def _():
        s_ref[...] = s0_ref[0, 0, 0] if has_s0 else jnp.zeros_like(s_ref)

    h = _modnorm(x_ref[0], gain_ref[...], sc_ref[0], sh_ref[0]).astype(BF)
    proj_ref[0] = _dot(h, w_ref[...])

    n_col = _positions(c_len)
    for hd in range(H_RET):
        lgf = lg_ref[0, hd]
        zeta = jnp.exp((c_len - 1.0 - n_col) * lgf)
        decay = jnp.exp(jnp.full((1, DV_RET), c_len, F32) * lgf)
        sstart_ref[0, 0, hd] = s_ref[hd].astype(BF)
        k = proj_ref[0, :, K_OFF + hd * DK_RET:K_OFF + (hd + 1) * DK_RET]
        v = proj_ref[0, :, V_OFF + hd * DV_RET:V_OFF + (hd + 1) * DV_RET]
        s_ref[hd] = s_ref[hd] * decay + _dot_tn((k * zeta).astype(BF), v.astype(BF))

    if emit_state:
        @pl.when(j == nt - 1)
        def _():
            sfin_ref[0] = s_ref[...]


def _in_proj(x3, gain, mod, w, lg, state):
    b, t, _ = x3.shape
    tm = TOKEN_TILE
    nt = t // tm
    has_s0 = state is not None
    emit_state = not has_s0
    in_specs = [pl.BlockSpec((1, tm, D_MODEL), lambda i, j, lg: (i, j, 0)), _vec_spec(D_MODEL),
                mod.spec_batch(1), mod.spec_batch(0), _resident(w.shape)]
    args = [x3, gain, mod.arr, mod.arr, w]
    if has_s0:
        in_specs.append(pl.BlockSpec((1, 1, 1, H_RET, DK_RET, DV_RET), lambda i, j, lg: (i, 0, 0, 0, 0, 0)))
        args.append(state)
    out_shape = [jax.ShapeDtypeStruct((b, t, EVEN_IN), F32),
                 jax.ShapeDtypeStruct((b, nt, H_RET, DK_RET, DV_RET), BF)]
    out_specs = [pl.BlockSpec((1, tm, EVEN_IN), lambda i, j, lg: (i, j, 0)),
                 pl.BlockSpec((1, 1, H_RET, DK_RET, DV_RET), lambda i, j, lg: (i, j, 0, 0, 0))]
    if emit_state:
        out_shape.append(jax.ShapeDtypeStruct((b, H_RET, DK_RET, DV_RET), F32))
        out_specs.append(pl.BlockSpec((1, H_RET, DK_RET, DV_RET), lambda i, j, lg: (i, 0, 0, 0)))
    return pl.pallas_call(
        functools.partial(_in_proj_kernel, nt=nt, has_s0=has_s0, emit_state=emit_state),
        out_shape=out_shape,
        grid_spec=pltpu.PrefetchScalarGridSpec(
            num_scalar_prefetch=1, grid=(b, nt), in_specs=in_specs, out_specs=out_specs,
            scratch_shapes=[pltpu.VMEM((H_RET, DK_RET, DV_RET), F32)]),
        compiler_params=_params("parallel", "arbitrary"),
        name="in_proj",
    )(lg, *args)


def _mixer_kernel(lg_ref, x_ref, proj_ref, prev_ref, next_ref, sstart_ref, gn_ref, pw_ref, ps_ref, wout_ref,
                  gain_ref, gate_ref, *rest, nt, t_len, has_s0, emit_state):
    rest = list(rest)
    s0_ref = rest.pop(0) if has_s0 else None
    o_ref = rest.pop(0)
    sfin_ref = rest.pop(0) if emit_state else None
    s_ref, pad_ref = rest
    j = pl.program_id(1)
    tile = nt - 1 - j
    c_len = TOKEN_TILE

    @pl.when(j == 0)
    def _():
        s_ref[...] = s0_ref[0, 0, 0] if has_s0 else jnp.zeros_like(s_ref)

    n_col = _positions(c_len)
    n_row = lax.broadcasted_iota(jnp.int32, (1, c_len), 1).astype(F32)
    diff = n_col - n_row
    qscale = DK_RET ** -0.5
    gn = gn_ref[...]
    pieces = []
    for hd in range(H_RET):
        lgf = lg_ref[0, hd]
        lgb = lg_ref[1, hd]
        dmask = (jnp.where(diff >= 0, jnp.exp(jnp.maximum(diff, 0.0) * lgf), 0.0)
                 + jnp.where(diff <= 0, jnp.exp(jnp.maximum(-diff, 0.0) * lgb), 0.0))
        xi_f = jnp.exp((n_col + 1.0) * lgf)
        xi_b = jnp.exp((c_len - n_col) * lgb)
        zeta_b = jnp.exp(n_col * lgb)
        decay_b = jnp.exp(jnp.full((1, DV_RET), c_len, F32) * lgb)

        q = proj_ref[0, :, Q_OFF + hd * DK_RET:Q_OFF + (hd + 1) * DK_RET] * qscale
        k = proj_ref[0, :, K_OFF + hd * DK_RET:K_OFF + (hd + 1) * DK_RET]
        v = proj_ref[0, :, V_OFF + hd * DV_RET:V_OFF + (hd + 1) * DV_RET].astype(BF)
        g = proj_ref[0, :, G_OFF + hd * DV_RET:G_OFF + (hd + 1) * DV_RET]
        scores = _dot_nt(q.astype(BF), k.astype(BF)) * dmask
        inner = _dot(scores.astype(BF), v)
        q_both = jnp.concatenate([(q * xi_f).astype(BF), (q * xi_b).astype(BF)], axis=1)
        s_both = jnp.concatenate([sstart_ref[0, 0, hd], s_ref[hd].astype(BF)], axis=0)
        y = inner + _dot(q_both, s_both)
        s_ref[hd] = s_ref[hd] * decay_b + _dot_tn((k * zeta_b).astype(BF), v)

        mu = jnp.mean(y, axis=-1, keepdims=True)
        yc = y - mu
        var = jnp.mean(yc * yc, axis=-1, keepdims=True)
        yn = (yc * lax.rsqrt(var + EPS)) * gn[:, hd * DV_RET:(hd + 1) * DV_RET]
        pieces.append((_silu(g) * yn).astype(BF))

    if emit_state:
        @pl.when(j == nt - 1)
        def _():
            sfin_ref[0] = s_ref[...]

    pad_ref[0:POOL_PAD, :] = jnp.where(tile > 0, prev_ref[0], 0.0)
    pad_ref[POOL_PAD:POOL_PAD + c_len, :] = proj_ref[0, :, U_OFF:U_OFF + POOL_WIDTH]
    pad_ref[POOL_PAD + c_len:2 * POOL_PAD + c_len, :] = jnp.where(tile < nt - 1, next_ref[0], 0.0)
    pos = tile * c_len + lax.broadcasted_iota(jnp.int32, (c_len, 1), 0)
    for grp, window in enumerate(POOL_WINDOWS):
        half = window // 2
        lanes = slice(grp * POOL_GROUP_DIM, (grp + 1) * POOL_GROUP_DIM)
        acc = pad_ref[POOL_PAD - half:POOL_PAD - half + c_len, lanes]
        for off in range(-half + 1, half):
            acc = acc + pad_ref[POOL_PAD + off:POOL_PAD + off + c_len, lanes]
        cnt = (jnp.minimum(pos + half, t_len) - jnp.maximum(pos - half, 0)).astype(F32)
        d = acc / cnt - pad_ref[POOL_PAD:POOL_PAD + c_len, lanes]
        pieces.append((_dot(d.astype(BF), pw_ref[grp]) * ps_ref[:, lanes]).astype(BF))

    o = _dot(jnp.concatenate(pieces, axis=1), wout_ref[...])
    o_ref[0] = x_ref[0] + gate_ref[0] * (_rms(o) * gain_ref[...])


def _mixer(x3, proj, sstart, mod, lg, gn_gain, pool_w, pool_scale, w_out, gain, state):
    b, t, _ = x3.shape
    tm = TOKEN_TILE
    nt = t // tm
    halo_per_tile = tm // POOL_PAD
    n_halo = t // POOL_PAD
    has_s0 = state is not None
    emit_state = not has_s0
    rev = lambda j: nt - 1 - j
    in_specs = [
        pl.BlockSpec((1, tm, D_MODEL), lambda i, j, lg: (i, rev(j), 0)),
        pl.BlockSpec((1, tm, EVEN_IN), lambda i, j, lg: (i, rev(j), 0)),
        pl.BlockSpec((1, POOL_PAD, POOL_WIDTH),
                     lambda i, j, lg: (i, jnp.maximum(rev(j) * halo_per_tile - 1, 0), U_OFF // POOL_WIDTH)),
        pl.BlockSpec((1, POOL_PAD, POOL_WIDTH),
                     lambda i, j, lg: (i, jnp.minimum((rev(j) + 1) * halo_per_tile, n_halo - 1), U_OFF // POOL_WIDTH)),
        pl.BlockSpec((1, 1, H_RET, DK_RET, DV_RET), lambda i, j, lg: (i, rev(j), 0, 0, 0)),
        _vec_spec(H_RET * DV_RET), _resident(pool_w.shape), _vec_spec(POOL_WIDTH), _resident(w_out.shape),
        _vec_spec(D_MODEL), mod.spec_batch(2),
    ]
    args = [x3, proj, proj, proj, sstart, gn_gain, pool_w, pool_scale, w_out, gain, mod.arr]
    if has_s0:
        in_specs.append(pl.BlockSpec((1, 1, 1, H_RET, DK_RET, DV_RET), lambda i, j, lg: (i, 0, 1, 0, 0, 0)))
        args.append(state)
    out_shape = [jax.ShapeDtypeStruct((b, t, D_MODEL), F32)]
    out_specs = [pl.BlockSpec((1, tm, D_MODEL), lambda i, j, lg: (i, rev(j), 0))]
    if emit_state:
        out_shape.append(jax.ShapeDtypeStruct((b, H_RET, DK_RET, DV_RET), F32))
        out_specs.append(pl.BlockSpec((1, H_RET, DK_RET, DV_RET), lambda i, j, lg: (i, 0, 0, 0)))
    outs = pl.pallas_call(
        functools.partial(_mixer_kernel, nt=nt, t_len=t, has_s0=has_s0, emit_state=emit_state),
        out_shape=out_shape,
        grid_spec=pltpu.PrefetchScalarGridSpec(
            num_scalar_prefetch=1, grid=(b, nt), in_specs=in_specs, out_specs=out_specs,
            scratch_shapes=[pltpu.VMEM((H_RET, DK_RET, DV_RET), F32),
                            pltpu.VMEM((tm + 2 * POOL_PAD, POOL_WIDTH), F32)]),
        compiler_params=_params("parallel", "arbitrary"),
        name="mixer",
    )(lg, *args)
    return outs if emit_state else (outs[0], None)


def _out_proj_kernel(x_ref, a_ref, w_ref, gain_ref, gate_ref, o_ref):
    o = _dot(a_ref[...].astype(BF), w_ref[...])
    o_ref[...] = x_ref[...] + gate_ref[0] * (_rms(o) * gain_ref[...])


def _out_proj(x, act, w, gain, mod):
    n_tok = x.shape[0]
    tm = TOKEN_TILE
    return pl.pallas_call(
        _out_proj_kernel,
        out_shape=jax.ShapeDtypeStruct((n_tok, D_MODEL), F32),
        grid=(n_tok // tm,),
        in_specs=[_row_spec(tm, D_MODEL), _row_spec(tm, act.shape[1]), _resident(w.shape), _vec_spec(D_MODEL),
                  mod.spec(2, tm)],
        out_specs=_row_spec(tm, D_MODEL),
        compiler_params=_params("parallel"),
        name="out_proj",
    )(x, act, w, gain, mod.arr)


def _ffn_kernel(x_ref, gain_in_ref, sc_ref, sh_ref, gate_ref, gain_out_ref, win_ref, wout_ref, o_ref):
    x = x_ref[...]
    h = _modnorm(x, gain_in_ref[...], sc_ref[0], sh_ref[0]).astype(BF)
    f = None
    for start, width in FF_CHUNKS:
        a = _dot(h, win_ref[:, start:start + width])
        b = _dot(h, win_ref[:, D_FF + start:D_FF + start + width])
        part = _dot((_silu(a) * b).astype(BF), wout_ref[start:start + width, :])
        f = part if f is None else f + part
    o_ref[...] = x + gate_ref[0] * (_rms(f) * gain_out_ref[...])


def _ffn(x, gain_in, gain_out, mod, w_in, w_out):
    n_tok = x.shape[0]
    tm = TOKEN_TILE
    return pl.pallas_call(
        _ffn_kernel,
        out_shape=jax.ShapeDtypeStruct((n_tok, D_MODEL), F32),
        grid=(n_tok // tm,),
        in_specs=[_row_spec(tm, D_MODEL), _vec_spec(D_MODEL), mod.spec(4, tm), mod.spec(3, tm), mod.spec(5, tm),
                  _vec_spec(D_MODEL), _resident(w_in.shape), _resident(w_out.shape)],
        out_specs=_row_spec(tm, D_MODEL),
        compiler_params=_params("parallel"),
        name="ffn",
    )(x, gain_in, mod.arr, mod.arr, mod.arr, gain_out, w_in, w_out)


def _rope(x, cs, sn):
    lane = lax.broadcasted_iota(jnp.int32, x.shape, 1)
    first = (lane % (2 * ROPE_PAIRS)) < ROPE_PAIRS
    partner = jnp.where(first, pltpu.roll(x, HEAD_DIM - ROPE_PAIRS, 1), pltpu.roll(x, ROPE_PAIRS, 1))
    return x * cs + partner * sn


def _qkv_kernel(x_ref, gain_ref, sc_ref, sh_ref, w_ref, qg_ref, kg_ref, *rest, rope):
    if rope:
        cs_ref, sn_ref, q_ref, k_ref, v_ref = rest
    else:
        q_ref, k_ref, v_ref = rest
    h = _modnorm(x_ref[...], gain_ref[...], sc_ref[0], sh_ref[0]).astype(BF)
    qkv = _dot(h, w_ref[...])

    def head(idx, g_ref):
        xh = _rms(qkv[:, idx * HEAD_DIM:(idx + 1) * HEAD_DIM]) * g_ref[...]
        return _rope(xh, cs_ref[...], sn_ref[...]) if rope else xh

    for i in range(N_HEADS):
        q_ref[:, i * HEAD_DIM:(i + 1) * HEAD_DIM] = head(i, qg_ref)
    for i in range(KV_HEADS):
        k_ref[:, i * HEAD_DIM:(i + 1) * HEAD_DIM] = head(N_HEADS + i, kg_ref)
    v_ref[...] = qkv[:, (N_HEADS + KV_HEADS) * HEAD_DIM:]


def _qkv(x, gain, mod, w, q_gain, k_gain, rope_tables):
    n_tok = x.shape[0]
    tm = TOKEN_TILE
    rope = rope_tables is not None
    in_specs = [_row_spec(tm, D_MODEL), _vec_spec(D_MODEL), mod.spec(1, tm), mod.spec(0, tm), _resident(w.shape),
                _vec_spec(HEAD_DIM), _vec_spec(HEAD_DIM)]
    args = [x, gain, mod.arr, mod.arr, w, q_gain, k_gain]
    if rope:
        tiles_per_seq = rope_tables[0].shape[0] // tm
        in_specs += [pl.BlockSpec((tm, HEAD_DIM), lambda i: (i % tiles_per_seq, 0))] * 2
        args += list(rope_tables)
    kv_w = KV_HEADS * HEAD_DIM
    return pl.pallas_call(
        functools.partial(_qkv_kernel, rope=rope),
        out_shape=[jax.ShapeDtypeStruct((n_tok, D_MODEL), F32), jax.ShapeDtypeStruct((n_tok, kv_w), F32),
                   jax.ShapeDtypeStruct((n_tok, kv_w), F32)],
        grid=(n_tok // tm,),
        in_specs=in_specs,
        out_specs=[_row_spec(tm, D_MODEL), _row_spec(tm, kv_w), _row_spec(tm, kv_w)],
        compiler_params=_params("parallel"),
        name="qkv_proj",
    )(*args)


def _rope_tables(t):
    pos = jnp.arange(t, dtype=jnp.int32)
    row = (pos // GRID_W).astype(F32)
    col = (pos % GRID_W).astype(F32)
    freqs = ROPE_BASE ** (-jnp.arange(ROPE_PAIRS, dtype=F32) / ROPE_PAIRS)
    ang_r = row[:, None] * freqs[None, :]
    ang_c = col[:, None] * freqs[None, :]
    cs = jnp.concatenate([jnp.cos(ang_r)] * 2 + [jnp.cos(ang_c)] * 2, axis=-1)
    sn = jnp.concatenate([-jnp.sin(ang_r), jnp.sin(ang_r), -jnp.sin(ang_c), jnp.sin(ang_c)], axis=-1)
    return cs, sn


def _attn_kernel(q_ref, k_ref, v_ref, o_ref):
    k = k_ref[0].astype(BF)
    v = v_ref[0].astype(BF)
    scale = HEAD_DIM ** -0.5
    for g in range(GQA):
        lanes = slice(g * HEAD_DIM, (g + 1) * HEAD_DIM)
        s = _dot_nt(q_ref[0, :, lanes].astype(BF), k) * scale
        e = jnp.exp(s - jnp.max(s, axis=-1, keepdims=True))
        denom = jnp.sum(e, axis=-1, keepdims=True)
        o_ref[0, :, lanes] = _dot(e.astype(BF), v) / denom


def _attention(q, k, v):
    b, t, _ = q.shape
    tk = k.shape[1]
    tq = 256
    gw = GQA * HEAD_DIM
    return pl.pallas_call(
        _attn_kernel,
        out_shape=jax.ShapeDtypeStruct(q.shape, F32),
        grid=(b, KV_HEADS, t // tq),
        in_specs=[
            pl.BlockSpec((1, tq, gw), lambda i, h, j: (i, j, h)),
            pl.BlockSpec((1, tk, HEAD_DIM), lambda i, h, j: (i, 0, h)),
            pl.BlockSpec((1, tk, HEAD_DIM), lambda i, h, j: (i, 0, h)),
        ],
        out_specs=pl.BlockSpec((1, tq, gw), lambda i, h, j: (i, j, h)),
        compiler_params=_params("parallel", "parallel", "parallel"),
        name="attention",
    )(q, k, v)


def _even_layer(x3, mod, norm_gain, w_in, lg, gn_gain, pool_w, pool_scale, w_out, state):
    proj, sstart, *s_fwd = _in_proj(x3, norm_gain[0:1], mod, w_in, lg, state)
    x3, s_bwd = _mixer(x3, proj, sstart, mod, lg, gn_gain, pool_w, pool_scale, w_out, norm_gain[1:2], state)
    s_fin = None if state is not None else jnp.stack([s_fwd[0], s_bwd], axis=1)
    return x3.reshape(-1, D_MODEL), s_fin


def _odd_layer(x, b, mod, norm_gain, w_qkv, q_gain, k_gain, w_o, cache_k, cache_v, rope_tables):
    t = x.shape[0] // b
    q, k, v = _qkv(x, norm_gain[0:1], mod, w_qkv, q_gain, k_gain, rope_tables)
    k3 = k.reshape(b, t, -1)
    v3 = v.reshape(b, t, -1)
    if cache_k is None:
        k_all, v_all = k3, v3
    else:
        k_all = jnp.concatenate([cache_k.reshape(b, -1, KV_HEADS * HEAD_DIM), k3], axis=1)
        v_all = jnp.concatenate([cache_v.reshape(b, -1, KV_HEADS * HEAD_DIM), v3], axis=1)
    att = _attention(q.reshape(b, t, -1), k_all, v_all).reshape(b * t, -1)
    x = _out_proj(x, att, w_o, norm_gain[1:2], mod)
    return x, k3, v3


def kernel(x_prompt, x_sample, state_ret, cache_k, cache_v, c, c_ctx, w_ada, b_ada, norm_gain, w_ffn_in, w_ffn_out,
           w_in_even, ret_decay_logit, ret_gn_gain, pool_w, pool_scale, w_out_even, w_qkv, q_norm_gain, k_norm_gain,
           w_o):
    bp, tp, _ = x_prompt.shape
    bs, ts, _ = x_sample.shape
    depth = w_ada.shape[0]
    assert bs + 1 <= MOD_ROWS and depth == 2

    cvec = jnp.zeros((MOD_ROWS, D_MODEL), F32).at[0].set(c_ctx).at[1:1 + bs].set(c)
    mod_flat = _modulation(cvec, w_ada, b_ada).reshape(depth * MOD_ROWS * MOD_PARTS, 1, D_MODEL)
    bf = lambda w: w.astype(BF)

    mod_p = _Mod(mod_flat, 0, None)
    mod_s = _Mod(mod_flat, 0, ts)
    lg = jax.nn.log_sigmoid(ret_decay_logit[0].astype(F32))
    even = functools.partial(_even_layer, norm_gain=norm_gain[0], w_in=bf(w_in_even[0]), lg=lg,
                             gn_gain=ret_gn_gain[0:1], pool_w=bf(pool_w[0]), pool_scale=pool_scale[0:1],
                             w_out=bf(w_out_even[0]))
    xp, s_fin = even(x_prompt, mod_p, state=None)
    xs, _ = even(x_sample, mod_s, state=state_ret)
    ffn0 = functools.partial(_ffn, gain_in=norm_gain[0, 2:3], gain_out=norm_gain[0, 3:4], w_in=bf(w_ffn_in[0]),
                             w_out=bf(w_ffn_out[0]))
    xp = ffn0(xp, mod=mod_p)
    xs = ffn0(xs, mod=mod_s)

    mod_p = _Mod(mod_flat, 1, None)
    mod_s = _Mod(mod_flat, 1, ts)
    odd = functools.partial(_odd_layer, norm_gain=norm_gain[1], w_qkv=bf(w_qkv[0]), q_gain=q_norm_gain[0:1],
                            k_gain=k_norm_gain[0:1], w_o=bf(w_o[0]))
    xp, kp, vp = odd(xp, bp, mod_p, cache_k=None, cache_v=None, rope_tables=None)
    xs, _, _ = odd(xs, bs, mod_s, cache_k=cache_k[:, 0], cache_v=cache_v[:, 0], rope_tables=_rope_tables(ts))
    ffn1 = functools.partial(_ffn, gain_in=norm_gain[1, 2:3], gain_out=norm_gain[1, 3:4], w_in=bf(w_ffn_in[1]),
                             w_out=bf(w_ffn_out[1]))
    xp = ffn1(xp, mod=mod_p)
    xs = ffn1(xs, mod=mod_s)

    new_state_ret = s_fin.reshape(bp, 1, 2, H_RET, DK_RET, DV_RET)
    new_cache_k = kp.reshape(bp, 1, tp, KV_HEADS, HEAD_DIM)
    new_cache_v = vp.reshape(bp, 1, tp, KV_HEADS, HEAD_DIM)
    return (xp.reshape(bp, tp, D_MODEL), xs.reshape(bs, ts, D_MODEL), new_state_ret, new_cache_k, new_cache_v)
```

```python
import functools

import jax
import jax.numpy as jnp
from jax import lax
from jax.experimental import pallas as pl
from jax.experimental.pallas import tpu as pltpu

D_MODEL = 1024
EPS = 1e-6
GRID_W = 64
H_RET = 4
DK_RET = 128
DV_RET = 256
POOL_WINDOWS = (2, 4, 8, 16)
POOL_GROUP_DIM = 128
POOL_WIDTH = 512
POOL_PAD = 8
Q_OFF, K_OFF, V_OFF, G_OFF, U_OFF = 0, 512, 1024, 2048, 3072
EVEN_IN = 3584
HEAD_DIM = 128
N_HEADS = 8
KV_HEADS = 2
GQA = N_HEADS // KV_HEADS
ROPE_BASE = 10000.0
ROPE_PAIRS = 32
QKV_OUT = (N_HEADS + 2 * KV_HEADS) * HEAD_DIM
D_FF = 2816
FF_CHUNKS = ((0, 1536), (1536, 1280))
MOD_ROWS = 8
MOD_PARTS = 6

BF = jnp.bfloat16
F32 = jnp.float32
VMEM_LIMIT_BYTES = 52 * 1024 * 1024
TOKEN_TILE = 256


def _params(*sem):
    return pltpu.CompilerParams(dimension_semantics=sem, vmem_limit_bytes=VMEM_LIMIT_BYTES)


def _resident(shape):
    nd = len(shape)
    return pl.BlockSpec(shape, lambda *_: (0,) * nd, pipeline_mode=pl.Buffered(1))


def _rms(x):
    return x * lax.rsqrt(jnp.mean(x * x, axis=-1, keepdims=True) + EPS)


def _modnorm(x, gain, sc, sh):
    return (_rms(x) * gain) * (1.0 + sc) + sh


def _silu(x):
    return x * jax.nn.sigmoid(x)


def _dot(a, b):
    return jnp.dot(a, b, preferred_element_type=F32)


def _dot_nt(a, b):
    return lax.dot_general(a, b, (((1,), (1,)), ((), ())), preferred_element_type=F32)


def _dot_tn(a, b):
    return lax.dot_general(a, b, (((0,), (0,)), ((), ())), preferred_element_type=F32)


def _mod_kernel(c_ref, w_ref, b_ref, o_ref):
    s = _silu(c_ref[...]).astype(BF)
    o_ref[0] = _dot(s, w_ref[0].astype(BF)) + b_ref[0]


def _modulation(cvec, w_ada, b_ada):
    depth = w_ada.shape[0]
    n = w_ada.shape[2]
    tn = 1536
    return pl.pallas_call(
        _mod_kernel,
        out_shape=jax.ShapeDtypeStruct((depth, MOD_ROWS, n), F32),
        grid=(depth, n // tn),
        in_specs=[
            pl.BlockSpec((MOD_ROWS, D_MODEL), lambda l, j: (0, 0)),
            pl.BlockSpec((1, D_MODEL, tn), lambda l, j: (l, 0, j)),
            pl.BlockSpec((1, 1, tn), lambda l, j: (l, 0, j)),
        ],
        out_specs=pl.BlockSpec((1, MOD_ROWS, tn), lambda l, j: (l, 0, j)),
        compiler_params=_params("parallel", "parallel"),
        name="adaln_mod",
    )(cvec, w_ada, b_ada.reshape(depth, 1, n))


class _Mod:
    def __init__(self, mod_flat, layer, tokens_per_row):
        self.arr = mod_flat
        self.layer = layer
        self.tokens_per_row = tokens_per_row

    def _block(self, row, part):
        return ((self.layer * MOD_ROWS + row) * MOD_PARTS + part, 0, 0)

    def spec(self, part, tile):
        def index(i, *_):
            return self._block(0 if self.tokens_per_row is None else 1 + (i * tile) // self.tokens_per_row, part)

        return pl.BlockSpec((1, 1, D_MODEL), index)

    def spec_batch(self, part):
        def index(b, *_):
            return self._block(0 if self.tokens_per_row is None else 1 + b, part)

        return pl.BlockSpec((1, 1, D_MODEL), index)


def _row_spec(tile, width):
    return pl.BlockSpec((tile, width), lambda i: (i, 0))


def _vec_spec(width):
    return pl.BlockSpec((1, width), lambda *_: (0, 0))


def _positions(n):
    return lax.broadcasted_iota(jnp.int32, (n, 1), 0).astype(F32)


def _in_proj_kernel(lg_ref, x_ref, gain_ref, sc_ref, sh_ref, w_ref, *rest, nt, has_s0, emit_state):
    rest = list(rest)
    s0_ref = rest.pop(0) if has_s0 else None
    proj_ref = rest.pop(0)
    sstart_ref = rest.pop(0)
    sfin_ref = rest.pop(0) if emit_state else None
    (s_ref,) = rest
    j = pl.program_id(1)
    c_len = TOKEN_TILE

    @pl.when(j == 0)
    def _():
        s_ref[...] = s0_ref[0, 0, 0] if has_s0 else jnp.zeros_like(s_ref)

    h = _modnorm(x_ref[0], gain_ref[...], sc_ref[0], sh_ref[0]).astype(BF)
    proj_ref[0] = _dot(h, w_ref[...])

    n_col = _positions(c_len)
    for hd in range(H_RET):
        lgf = lg_ref[0, hd]
        zeta = jnp.exp((c_len - 1.0 - n_col) * lgf)
        decay = jnp.exp(jnp.full((1, DV_RET), c_len, F32) * lgf)
        sstart_ref[0, 0, hd] = s_ref[hd].astype(BF)
        k = proj_ref[0, :, K_OFF + hd * DK_RET:K_OFF + (hd + 1) * DK_RET]
        v = proj_ref[0, :, V_OFF + hd * DV_RET:V_OFF + (hd + 1) * DV_RET]
        s_ref[hd] = s_ref[hd] * decay + _dot_tn((k * zeta).astype(BF), v.astype(BF))

    if emit_state:
        @pl.when(j == nt - 1)
        def _():
            sfin_ref[0] = s_ref[...]


def _in_proj(x3, gain, mod, w, lg, state):
    b, t, _ = x3.shape
    tm = TOKEN_TILE
    nt = t // tm
    has_s0 = state is not None
    emit_state = not has_s0
    in_specs = [pl.BlockSpec((1, tm, D_MODEL), lambda i, j, lg: (i, j, 0)), _vec_spec(D_MODEL),
                mod.spec_batch(1), mod.spec_batch(0), _resident(w.shape)]
    args = [x3, gain, mod.arr, mod.arr, w]
    if has_s0:
        in_specs.append(pl.BlockSpec((1, 1, 1, H_RET, DK_RET, DV_RET), lambda i, j, lg: (i, 0, 0, 0, 0, 0)))
        args.append(state)
    out_shape = [jax.ShapeDtypeStruct((b, t, EVEN_IN), F32),
                 jax.ShapeDtypeStruct((b, nt, H_RET, DK_RET, DV_RET), BF)]
    out_specs = [pl.BlockSpec((1, tm, EVEN_IN), lambda i, j, lg: (i, j, 0)),
                 pl.BlockSpec((1, 1, H_RET, DK_RET, DV_RET), lambda i, j, lg: (i, j, 0, 0, 0))]
    if emit_state:
        out_shape.append(jax.ShapeDtypeStruct((b, H_RET, DK_RET, DV_RET), F32))
        out_specs.append(pl.BlockSpec((1, H_RET, DK_RET, DV_RET), lambda i, j, lg: (i, 0, 0, 0)))
    return pl.pallas_call(
        functools.partial(_in_proj_kernel, nt=nt, has_s0=has_s0, emit_state=emit_state),
        out_shape=out_shape,
        grid_spec=pltpu.PrefetchScalarGridSpec(
            num_scalar_prefetch=1, grid=(b, nt), in_specs=in_specs, out_specs=out_specs,
            scratch_shapes=[pltpu.VMEM((H_RET, DK_RET, DV_RET), F32)]),
        compiler_params=_params("parallel", "arbitrary"),
        name="in_proj",
    )(lg, *args)


def _mixer_kernel(lg_ref, x_ref, proj_ref, prev_ref, next_ref, sstart_ref, gn_ref, pw_ref, ps_ref, wout_ref,
                  gain_ref, gate_ref, *rest, nt, t_len, has_s0, emit_state):
    rest = list(rest)
    s0_ref = rest.pop(0) if has_s0 else None
    o_ref = rest.pop(0)
    sfin_ref = rest.pop(0) if emit_state else None
    s_ref, pad_ref = rest
    j = pl.program_id(1)
    tile = nt - 1 - j
    c_len = TOKEN_TILE

    @pl.when(j == 0)
    def _():
        s_ref[...] = s0_ref[0, 0, 0] if has_s0 else jnp.zeros_like(s_ref)

    n_col = _positions(c_len)
    n_row = lax.broadcasted_iota(jnp.int32, (1, c_len), 1).astype(F32)
    diff = n_col - n_row
    qscale = DK_RET ** -0.5
    gn = gn_ref[...]
    pieces = []
    for hd in range(H_RET):
        lgf = lg_ref[0, hd]
        lgb = lg_ref[1, hd]
        dmask = (jnp.where(diff >= 0, jnp.exp(jnp.maximum(diff, 0.0) * lgf), 0.0)
                 + jnp.where(diff <= 0, jnp.exp(jnp.maximum(-diff, 0.0) * lgb), 0.0))
        xi_f = jnp.exp((n_col + 1.0) * lgf)
        xi_b = jnp.exp((c_len - n_col) * lgb)
        zeta_b = jnp.exp(n_col * lgb)
        decay_b = jnp.exp(jnp.full((1, DV_RET), c_len, F32) * lgb)

        q = proj_ref[0, :, Q_OFF + hd * DK_RET:Q_OFF + (hd + 1) * DK_RET] * qscale
        k = proj_ref[0, :, K_OFF + hd * DK_RET:K_OFF + (hd + 1) * DK_RET]
        v = proj_ref[0, :, V_OFF + hd * DV_RET:V_OFF + (hd + 1) * DV_RET].astype(BF)
        g = proj_ref[0, :, G_OFF + hd * DV_RET:G_OFF + (hd + 1) * DV_RET]
        scores = _dot_nt(q.astype(BF), k.astype(BF)) * dmask
        inner = _dot(scores.astype(BF), v)
        q_both = jnp.concatenate([(q * xi_f).astype(BF), (q * xi_b).astype(BF)], axis=1)
        s_both = jnp.concatenate([sstart_ref[0, 0, hd], s_ref[hd].astype(BF)], axis=0)
        y = inner + _dot(q_both, s_both)
        s_ref[hd] = s_ref[hd] * decay_b + _dot_tn((k * zeta_b).astype(BF), v)

        mu = jnp.mean(y, axis=-1, keepdims=True)
        yc = y - mu
        var = jnp.mean(yc * yc, axis=-1, keepdims=True)
        yn = (yc * lax.rsqrt(var + EPS)) * gn[:, hd * DV_RET:(hd + 1) * DV_RET]
        pieces.append((_silu(g) * yn).astype(BF))

    if emit_state:
        @pl.when(j == nt - 1)
        def _():
            sfin_ref[0] = s_ref[...]

    pad_ref[0:POOL_PAD, :] = jnp.where(tile > 0, prev_ref[0], 0.0)
    pad_ref[POOL_PAD:POOL_PAD + c_len, :] = proj_ref[0, :, U_OFF:U_OFF + POOL_WIDTH]
    pad_ref[POOL_PAD + c_len:2 * POOL_PAD + c_len, :] = jnp.where(tile < nt - 1, next_ref[0], 0.0)
    pos = tile * c_len + lax.broadcasted_iota(jnp.int32, (c_len, 1), 0)
    for grp, window in enumerate(POOL_WINDOWS):
        half = window // 2
        lanes = slice(grp * POOL_GROUP_DIM, (grp + 1) * POOL_GROUP_DIM)
        acc = pad_ref[POOL_PAD - half:POOL_PAD - half + c_len, lanes]
        for off in range(-half + 1, half):
            acc = acc + pad_ref[POOL_PAD + off:POOL_PAD + off + c_len, lanes]
        cnt = (jnp.minimum(pos + half, t_len) - jnp.maximum(pos - half, 0)).astype(F32)
        d = acc / cnt - pad_ref[POOL_PAD:POOL_PAD + c_len, lanes]
        pieces.append((_dot(d.astype(BF), pw_ref[grp]) * ps_ref[:, lanes]).astype(BF))

    o = _dot(jnp.concatenate(pieces, axis=1), wout_ref[...])
    o_ref[0] = x_ref[0] + gate_ref[0] * (_rms(o) * gain_ref[...])


def _mixer(x3, proj, sstart, mod, lg, gn_gain, pool_w, pool_scale, w_out, gain, state):
    b, t, _ = x3.shape
    tm = TOKEN_TILE
    nt = t // tm
    halo_per_tile = tm // POOL_PAD
    n_halo = t // POOL_PAD
    has_s0 = state is not None
    emit_state = not has_s0
    rev = lambda j: nt - 1 - j
    in_specs = [
        pl.BlockSpec((1, tm, D_MODEL), lambda i, j, lg: (i, rev(j), 0)),
        pl.BlockSpec((1, tm, EVEN_IN), lambda i, j, lg: (i, rev(j), 0)),
        pl.BlockSpec((1, POOL_PAD, POOL_WIDTH),
                     lambda i, j, lg: (i, jnp.maximum(rev(j) * halo_per_tile - 1, 0), U_OFF // POOL_WIDTH)),
        pl.BlockSpec((1, POOL_PAD, POOL_WIDTH),
                     lambda i, j, lg: (i, jnp.minimum((rev(j) + 1) * halo_per_tile, n_halo - 1), U_OFF // POOL_WIDTH)),
        pl.BlockSpec((1, 1, H_RET, DK_RET, DV_RET), lambda i, j, lg: (i, rev(j), 0, 0, 0)),
        _vec_spec(H_RET * DV_RET), _resident(pool_w.shape), _vec_spec(POOL_WIDTH), _resident(w_out.shape),
        _vec_spec(D_MODEL), mod.spec_batch(2),
    ]
    args = [x3, proj, proj, proj, sstart, gn_gain, pool_w, pool_scale, w_out, gain, mod.arr]
    if has_s0:
        in_specs.append(pl.BlockSpec((1, 1, 1, H_RET, DK_RET, DV_RET), lambda i, j, lg: (i, 0, 1, 0, 0, 0)))
        args.append(state)
    out_shape = [jax.ShapeDtypeStruct((b, t, D_MODEL), F32)]
    out_specs = [pl.BlockSpec((1, tm, D_MODEL), lambda i, j, lg: (i, rev(j), 0))]
    if emit_state:
        out_shape.append(jax.ShapeDtypeStruct((b, H_RET, DK_RET, DV_RET), F32))
        out_specs.append(pl.BlockSpec((1, H_RET, DK_RET, DV_RET), lambda i, j, lg: (i, 0, 0, 0)))
    outs = pl.pallas_call(
        functools.partial(_mixer_kernel, nt=nt, t_len=t, has_s0=has_s0, emit_state=emit_state),
        out_shape=out_shape,
        grid_spec=pltpu.PrefetchScalarGridSpec(
            num_scalar_prefetch=1, grid=(b, nt), in_specs=in_specs, out_specs=out_specs,
            scratch_shapes=[pltpu.VMEM((H_RET, DK_RET, DV_RET), F32),
                            pltpu.VMEM((tm + 2 * POOL_PAD, POOL_WIDTH), F32)]),
        compiler_params=_params("parallel", "arbitrary"),
        name="mixer",
    )(lg, *args)
    return outs if emit_state else (outs[0], None)


def _ffn_kernel(x_ref, gain_in_ref, sc_ref, sh_ref, gate_ref, gain_out_ref, win_ref, wout_ref, o_ref):
    x = x_ref[...]
    h = _modnorm(x, gain_in_ref[...], sc_ref[0], sh_ref[0]).astype(BF)
    f = None
    for start, width in FF_CHUNKS:
        a = _dot(h, win_ref[:, start:start + width])
        b = _dot(h, win_ref[:, D_FF + start:D_FF + start + width])
        part = _dot((_silu(a) * b).astype(BF), wout_ref[start:start + width, :])
        f = part if f is None else f + part
    o_ref[...] = x + gate_ref[0] * (_rms(f) * gain_out_ref[...])


def _ffn(x, gain_in, gain_out, mod, w_in, w_out):
    n_tok = x.shape[0]
    tm = TOKEN_TILE
    return pl.pallas_call(
        _ffn_kernel,
        out_shape=jax.ShapeDtypeStruct((n_tok, D_MODEL), F32),
        grid=(n_tok // tm,),
        in_specs=[_row_spec(tm, D_MODEL), _vec_spec(D_MODEL), mod.spec(4, tm), mod.spec(3, tm), mod.spec(5, tm),
                  _vec_spec(D_MODEL), _resident(w_in.shape), _resident(w_out.shape)],
        out_specs=_row_spec(tm, D_MODEL),
        compiler_params=_params("parallel"),
        name="ffn",
    )(x, gain_in, mod.arr, mod.arr, mod.arr, gain_out, w_in, w_out)


def _rope(x, cs, sn):
    lane = lax.broadcasted_iota(jnp.int32, x.shape, 1)
    first = (lane % (2 * ROPE_PAIRS)) < ROPE_PAIRS
    partner = jnp.where(first, pltpu.roll(x, HEAD_DIM - ROPE_PAIRS, 1), pltpu.roll(x, ROPE_PAIRS, 1))
    return x * cs + partner * sn


Q_PRESCALE = HEAD_DIM ** -0.5 * 1.4426950408889634


def _qkv_kernel(x_ref, gain_ref, sc_ref, sh_ref, w_ref, qg_ref, kg_ref, *rest, rope, emit_cache):
    rest = list(rest)
    cs_ref = rest.pop(0) if rope else None
    sn_ref = rest.pop(0) if rope else None
    q_ref, kb_ref, vb_ref = rest[:3]
    kf_ref, vf_ref = rest[3:] if emit_cache else (None, None)
    h = _modnorm(x_ref[...], gain_ref[...], sc_ref[0], sh_ref[0]).astype(BF)
    qkv = _dot(h, w_ref[...])

    def head(idx, g_ref):
        xh = _rms(qkv[:, idx * HEAD_DIM:(idx + 1) * HEAD_DIM]) * g_ref[...]
        return _rope(xh, cs_ref[...], sn_ref[...]) if rope else xh

    for i in range(N_HEADS):
        q_ref[:, i * HEAD_DIM:(i + 1) * HEAD_DIM] = (head(i, qg_ref) * Q_PRESCALE).astype(BF)
    for i in range(KV_HEADS):
        lanes = slice(i * HEAD_DIM, (i + 1) * HEAD_DIM)
        kh = head(N_HEADS + i, kg_ref)
        kb_ref[:, lanes] = kh.astype(BF)
        if emit_cache:
            kf_ref[:, lanes] = kh
    v = qkv[:, (N_HEADS + KV_HEADS) * HEAD_DIM:]
    vb_ref[...] = v.astype(BF)
    if emit_cache:
        vf_ref[...] = v


def _qkv(x, gain, mod, w, q_gain, k_gain, rope_tables, emit_cache):
    n_tok = x.shape[0]
    tm = TOKEN_TILE
    rope = rope_tables is not None
    in_specs = [_row_spec(tm, D_MODEL), _vec_spec(D_MODEL), mod.spec(1, tm), mod.spec(0, tm), _resident(w.shape),
                _vec_spec(HEAD_DIM), _vec_spec(HEAD_DIM)]
    args = [x, gain, mod.arr, mod.arr, w, q_gain, k_gain]
    if rope:
        tiles_per_seq = rope_tables[0].shape[0] // tm
        in_specs += [pl.BlockSpec((tm, HEAD_DIM), lambda i: (i % tiles_per_seq, 0))] * 2
        args += list(rope_tables)
    kv_w = KV_HEADS * HEAD_DIM
    widths = [(D_MODEL, BF), (kv_w, BF), (kv_w, BF)] + ([(kv_w, F32), (kv_w, F32)] if emit_cache else [])
    return pl.pallas_call(
        functools.partial(_qkv_kernel, rope=rope, emit_cache=emit_cache),
        out_shape=[jax.ShapeDtypeStruct((n_tok, wd), dt) for wd, dt in widths],
        grid=(n_tok // tm,),
        in_specs=in_specs,
        out_specs=[_row_spec(tm, wd) for wd, _ in widths],
        compiler_params=_params("parallel"),
        name="qkv_proj",
    )(*args)


def _rope_tables(t):
    pos = jnp.arange(t, dtype=jnp.int32)
    row = (pos // GRID_W).astype(F32)
    col = (pos % GRID_W).astype(F32)
    freqs = ROPE_BASE ** (-jnp.arange(ROPE_PAIRS, dtype=F32) / ROPE_PAIRS)
    ang_r = row[:, None] * freqs[None, :]
    ang_c = col[:, None] * freqs[None, :]
    cs = jnp.concatenate([jnp.cos(ang_r)] * 2 + [jnp.cos(ang_c)] * 2, axis=-1)
    sn = jnp.concatenate([-jnp.sin(ang_r), jnp.sin(ang_r), -jnp.sin(ang_c), jnp.sin(ang_c)], axis=-1)
    return cs, sn


KV_BLOCK = 512
LANES = 128


def _fold_lanes(x, op):
    out = x[:, :LANES]
    for i in range(1, x.shape[1] // LANES):
        out = op(out, x[:, i * LANES:(i + 1) * LANES])
    return out


def _attn_kernel(x_ref, q_ref, *rest, seg_lens, blk):
    n_seg = len(seg_lens)
    kv_refs = rest[:2 * n_seg]
    w_ref, gain_ref, gate_ref, o_ref, s_ref, m_ref, acc_ref = rest[2 * n_seg:]
    tq = q_ref.shape[1]
    kv_lanes = [slice(kvh * HEAD_DIM, (kvh + 1) * HEAD_DIM) for kvh in range(KV_HEADS)]

    def over_blocks(body):
        base = 0
        for seg, n in enumerate(seg_lens):
            nb = n // blk
            if nb == 1:
                body(seg, 0, base)
            else:
                def step(i, carry, seg=seg, base=base):
                    body(seg, pl.multiple_of(i * blk, blk), base + i)
                    return carry

                lax.fori_loop(0, nb, step, 0, unroll=True)
            base += nb

    m_ref[...] = jnp.full(m_ref.shape, -jnp.inf, F32)

    def scores(seg, row0, b):
        for kvh in range(KV_HEADS):
            q4 = jnp.concatenate([q_ref[0, :, (kvh * GQA + g) * HEAD_DIM:(kvh * GQA + g + 1) * HEAD_DIM]
                                  for g in range(GQA)], axis=0)
            k = kv_refs[2 * seg][0, pl.ds(row0, blk), kv_lanes[kvh]].astype(BF)
            s = _dot_nt(q4, k)
            s_ref[kvh, b] = s
            m_ref[kvh] = jnp.maximum(m_ref[kvh], _fold_lanes(s, jnp.maximum))

    over_blocks(scores)
    for kvh in range(KV_HEADS):
        m_ref[kvh] = jnp.broadcast_to(jnp.max(m_ref[kvh], axis=-1, keepdims=True), m_ref.shape[1:])

    acc_ref[...] = jnp.zeros(acc_ref.shape, F32)
    ones = jnp.ones((blk, LANES), BF)

    def values(seg, row0, b):
        for kvh in range(KV_HEADS):
            v = kv_refs[2 * seg + 1][0, pl.ds(row0, blk), kv_lanes[kvh]].astype(BF)
            row_max = m_ref[kvh]
            e = jnp.concatenate(
                [jnp.exp2(s_ref[kvh, b, :, t * LANES:(t + 1) * LANES] - row_max).astype(BF)
                 for t in range(blk // LANES)], axis=1)
            acc_ref[kvh] += _dot(e, jnp.concatenate([v, ones], axis=1))

    over_blocks(values)
    pieces = []
    for kvh in range(KV_HEADS):
        att = acc_ref[kvh, :, :HEAD_DIM] / acc_ref[kvh, :, HEAD_DIM:]
        pieces += [att[g * tq:(g + 1) * tq].astype(BF) for g in range(GQA)]

    o = _dot(jnp.concatenate(pieces, axis=1), w_ref[...])
    o_ref[0] = x_ref[0] + gate_ref[0] * (_rms(o) * gain_ref[...])


def _attention(x3, q3, kv_segments, w_o, gain, mod):
    b, t, _ = x3.shape
    tq = TOKEN_TILE
    seg_lens = tuple(k.shape[1] for k, _ in kv_segments)
    blk = min(KV_BLOCK, *seg_lens)
    assert all(n % blk == 0 for n in seg_lens)
    n_blocks = sum(seg_lens) // blk
    rows = GQA * tq
    kv_w = KV_HEADS * HEAD_DIM
    in_specs = [pl.BlockSpec((1, tq, D_MODEL), lambda i, j: (i, j, 0)),
                pl.BlockSpec((1, tq, D_MODEL), lambda i, j: (i, j, 0))]
    args = [x3, q3]
    for k, v in kv_segments:
        in_specs += [pl.BlockSpec((1, k.shape[1], kv_w), lambda i, j: (i, 0, 0))] * 2
        args += [k, v]
    in_specs += [_resident(w_o.shape), _vec_spec(D_MODEL), mod.spec_batch(2)]
    args += [w_o, gain, mod.arr]
    return pl.pallas_call(
        functools.partial(_attn_kernel, seg_lens=seg_lens, blk=blk),
        out_shape=jax.ShapeDtypeStruct(x3.shape, F32),
        grid=(b, t // tq),
        in_specs=in_specs,
        out_specs=pl.BlockSpec((1, tq, D_MODEL), lambda i, j: (i, j, 0)),
        scratch_shapes=[pltpu.VMEM((KV_HEADS, n_blocks, rows, blk), F32),
                        pltpu.VMEM((KV_HEADS, rows, LANES), F32),
                        pltpu.VMEM((KV_HEADS, rows, HEAD_DIM + LANES), F32)],
        compiler_params=_params("parallel", "parallel"),
        name="attention",
    )(*args)


def _even_layer(x3, mod, norm_gain, w_in, lg, gn_gain, pool_w, pool_scale, w_out, state):
    proj, sstart, *s_fwd = _in_proj(x3, norm_gain[0:1], mod, w_in, lg, state)
    x3, s_bwd = _mixer(x3, proj, sstart, mod, lg, gn_gain, pool_w, pool_scale, w_out, norm_gain[1:2], state)
    s_fin = None if state is not None else jnp.stack([s_fwd[0], s_bwd], axis=1)
    return x3.reshape(-1, D_MODEL), s_fin


def _odd_layer(x, b, mod, norm_gain, w_qkv, q_gain, k_gain, w_o, cache, rope_tables):
    t = x.shape[0] // b
    emit_cache = cache is None
    q, kb, vb, *kv_f32 = _qkv(x, norm_gain[0:1], mod, w_qkv, q_gain, k_gain, rope_tables, emit_cache)
    as3 = lambda a: a.reshape(b, t, -1)
    segments = [(as3(kb), as3(vb))]
    if cache is not None:
        segments.insert(0, tuple(a.reshape(b, -1, KV_HEADS * HEAD_DIM) for a in cache))
    x3 = _attention(as3(x), as3(q), segments, w_o, norm_gain[1:2], mod)
    return x3.reshape(-1, D_MODEL), kv_f32


def kernel(x_prompt, x_sample, state_ret, cache_k, cache_v, c, c_ctx, w_ada, b_ada, norm_gain, w_ffn_in, w_ffn_out,
           w_in_even, ret_decay_logit, ret_gn_gain, pool_w, pool_scale, w_out_even, w_qkv, q_norm_gain, k_norm_gain,
           w_o):
    bp, tp, _ = x_prompt.shape
    bs, ts, _ = x_sample.shape
    depth = w_ada.shape[0]
    assert bs + 1 <= MOD_ROWS and depth == 2

    cvec = jnp.zeros((MOD_ROWS, D_MODEL), F32).at[0].set(c_ctx).at[1:1 + bs].set(c)
    mod_flat = _modulation(cvec, w_ada, b_ada).reshape(depth * MOD_ROWS * MOD_PARTS, 1, D_MODEL)
    bf = lambda w: w.astype(BF)

    mod_p = _Mod(mod_flat, 0, None)
    mod_s = _Mod(mod_flat, 0, ts)
    lg = jax.nn.log_sigmoid(ret_decay_logit[0].astype(F32))
    even = functools.partial(_even_layer, norm_gain=norm_gain[0], w_in=bf(w_in_even[0]), lg=lg,
                             gn_gain=ret_gn_gain[0:1], pool_w=bf(pool_w[0]), pool_scale=pool_scale[0:1],
                             w_out=bf(w_out_even[0]))
    xp, s_fin = even(x_prompt, mod_p, state=None)
    xs, _ = even(x_sample, mod_s, state=state_ret)
    ffn0 = functools.partial(_ffn, gain_in=norm_gain[0, 2:3], gain_out=norm_gain[0, 3:4], w_in=bf(w_ffn_in[0]),
                             w_out=bf(w_ffn_out[0]))
    xp = ffn0(xp, mod=mod_p)
    xs = ffn0(xs, mod=mod_s)

    mod_p = _Mod(mod_flat, 1, None)
    mod_s = _Mod(mod_flat, 1, ts)
    odd = functools.partial(_odd_layer, norm_gain=norm_gain[1], w_qkv=bf(w_qkv[0]), q_gain=q_norm_gain[0:1],
                            k_gain=k_norm_gain[0:1], w_o=bf(w_o[0]))
    xp, (kp, vp) = odd(xp, bp, mod_p, cache=None, rope_tables=None)
    xs, _ = odd(xs, bs, mod_s, cache=(cache_k[:, 0], cache_v[:, 0]), rope_tables=_rope_tables(ts))
    ffn1 = functools.partial(_ffn, gain_in=norm_gain[1, 2:3], gain_out=norm_gain[1, 3:4], w_in=bf(w_ffn_in[1]),
                             w_out=bf(w_ffn_out[1]))
    xp = ffn1(xp, mod=mod_p)
    xs = ffn1(xs, mod=mod_s)

    new_state_ret = s_fin.reshape(bp, 1, 2, H_RET, DK_RET, DV_RET)
    new_cache_k = kp.reshape(bp, 1, tp, KV_HEADS, HEAD_DIM)
    new_cache_v = vp.reshape(bp, 1, tp, KV_HEADS, HEAD_DIM)
    return (xp.reshape(bp, tp, D_MODEL), xs.reshape(bs, ts, D_MODEL), new_state_ret, new_cache_k, new_cache_v)
```

```python
import functools

import jax
import jax.numpy as jnp
from jax import lax
from jax.experimental import pallas as pl
from jax.experimental.pallas import tpu as pltpu

D_MODEL = 1024
EPS = 1e-6
GRID_W = 64
H_RET = 4
DK_RET = 128
DV_RET = 256
POOL_WINDOWS = (2, 4, 8, 16)
POOL_GROUP_DIM = 128
POOL_WIDTH = 512
POOL_PAD = 8
Q_OFF, K_OFF, V_OFF, G_OFF, U_OFF = 0, 512, 1024, 2048, 3072
EVEN_IN = 3584
HEAD_DIM = 128
N_HEADS = 8
KV_HEADS = 2
GQA = N_HEADS // KV_HEADS
ROPE_BASE = 10000.0
ROPE_PAIRS = 32
QKV_OUT = (N_HEADS + 2 * KV_HEADS) * HEAD_DIM
D_FF = 2816
FF_CHUNKS = ((0, 1536), (1536, 1280))
MOD_ROWS = 8
MOD_PARTS = 6

BF = jnp.bfloat16
F32 = jnp.float32
VMEM_LIMIT_BYTES = 52 * 1024 * 1024
TOKEN_TILE = 256
FFN_TILE = 512


def _params(*sem):
    return pltpu.CompilerParams(dimension_semantics=sem, vmem_limit_bytes=VMEM_LIMIT_BYTES)


def _resident(shape):
    nd = len(shape)
    return pl.BlockSpec(shape, lambda *_: (0,) * nd, pipeline_mode=pl.Buffered(1))


def _rms(x):
    return x * lax.rsqrt(jnp.mean(x * x, axis=-1, keepdims=True) + EPS)


def _modnorm(x, gain, sc, sh):
    return (_rms(x) * gain) * (1.0 + sc) + sh


def _silu(x):
    return x * jax.nn.sigmoid(x)


def _dot(a, b):
    return jnp.dot(a, b, preferred_element_type=F32)


def _dot_nt(a, b):
    return lax.dot_general(a, b, (((1,), (1,)), ((), ())), preferred_element_type=F32)


def _dot_tn(a, b):
    return lax.dot_general(a, b, (((0,), (0,)), ((), ())), preferred_element_type=F32)


def _mod_kernel(c_ref, w_ref, b_ref, o_ref):
    s = _silu(c_ref[...]).astype(BF)
    o_ref[0] = _dot(s, w_ref[0].astype(BF)) + b_ref[0]


def _modulation(cvec, w_ada, b_ada):
    depth = w_ada.shape[0]
    n = w_ada.shape[2]
    tn = 1536
    return pl.pallas_call(
        _mod_kernel,
        out_shape=jax.ShapeDtypeStruct((depth, MOD_ROWS, n), F32),
        grid=(depth, n // tn),
        in_specs=[
            pl.BlockSpec((MOD_ROWS, D_MODEL), lambda l, j: (0, 0)),
            pl.BlockSpec((1, D_MODEL, tn), lambda l, j: (l, 0, j)),
            pl.BlockSpec((1, 1, tn), lambda l, j: (l, 0, j)),
        ],
        out_specs=pl.BlockSpec((1, MOD_ROWS, tn), lambda l, j: (l, 0, j)),
        compiler_params=_params("parallel", "parallel"),
        name="adaln_mod",
    )(cvec, w_ada, b_ada.reshape(depth, 1, n))


class _Mod:
    def __init__(self, mod_flat, layer, tokens_per_row):
        self.arr = mod_flat
        self.layer = layer
        self.tokens_per_row = tokens_per_row

    def _block(self, row, part):
        return ((self.layer * MOD_ROWS + row) * MOD_PARTS + part, 0, 0)

    def spec(self, part, tile):
        def index(i, *_):
            return self._block(0 if self.tokens_per_row is None else 1 + (i * tile) // self.tokens_per_row, part)

        return pl.BlockSpec((1, 1, D_MODEL), index)

    def spec_batch(self, part):
        def index(b, *_):
            return self._block(0 if self.tokens_per_row is None else 1 + b, part)

        return pl.BlockSpec((1, 1, D_MODEL), index)


def _row_spec(tile, width):
    return pl.BlockSpec((tile, width), lambda i: (i, 0))


def _vec_spec(width):
    return pl.BlockSpec((1, width), lambda *_: (0, 0))


def _positions(n):
    return lax.broadcasted_iota(jnp.int32, (n, 1), 0).astype(F32)


XI_F, XI_B, ZETA_F, ZETA_B = range(4)


def _decay_kernel(lg_ref, dmask_ref, vec_ref, dec_ref):
    hd = pl.program_id(0)
    lgf = lg_ref[0, hd]
    lgb = lg_ref[1, hd]
    c_len = TOKEN_TILE
    qscale = DK_RET ** -0.5
    diff = _positions(c_len) - lax.broadcasted_iota(jnp.int32, (1, c_len), 1).astype(F32)
    dmask_ref[0] = qscale * (jnp.where(diff >= 0, jnp.exp(jnp.maximum(diff, 0.0) * lgf), 0.0)
                             + jnp.where(diff <= 0, jnp.exp(jnp.maximum(-diff, 0.0) * lgb), 0.0))
    n = lax.broadcasted_iota(jnp.int32, (c_len, DK_RET), 0).astype(F32)
    vec_ref[0, XI_F] = qscale * jnp.exp((n + 1.0) * lgf)
    vec_ref[0, XI_B] = qscale * jnp.exp((c_len - n) * lgb)
    vec_ref[0, ZETA_F] = jnp.exp((c_len - 1.0 - n) * lgf)
    vec_ref[0, ZETA_B] = jnp.exp(n * lgb)
    chunk = jnp.full((1, DV_RET), c_len, F32)
    dec_ref[0, 0] = jnp.exp(chunk * lgf)
    dec_ref[0, 1] = jnp.exp(chunk * lgb)


def _decay_tables(lg):
    c_len = TOKEN_TILE
    return pl.pallas_call(
        _decay_kernel,
        out_shape=[jax.ShapeDtypeStruct((H_RET, c_len, c_len), F32),
                   jax.ShapeDtypeStruct((H_RET, 4, c_len, DK_RET), F32),
                   jax.ShapeDtypeStruct((H_RET, 2, 1, DV_RET), F32)],
        grid_spec=pltpu.PrefetchScalarGridSpec(
            num_scalar_prefetch=1, grid=(H_RET,), in_specs=[],
            out_specs=[pl.BlockSpec((1, c_len, c_len), lambda h, lg: (h, 0, 0)),
                       pl.BlockSpec((1, 4, c_len, DK_RET), lambda h, lg: (h, 0, 0, 0)),
                       pl.BlockSpec((1, 2, 1, DV_RET), lambda h, lg: (h, 0, 0, 0))]),
        compiler_params=_params("arbitrary"),
        name="decay_tables",
    )(lg)


def _in_proj_kernel(x_ref, gain_ref, sc_ref, sh_ref, w_ref, vec_ref, dec_ref, *rest, nt, has_s0, emit_state):
    rest = list(rest)
    s0_ref = rest.pop(0) if has_s0 else None
    proj_ref = rest.pop(0)
    sstart_ref = rest.pop(0)
    sfin_ref = rest.pop(0) if emit_state else None
    (s_ref,) = rest
    j = pl.program_id(1)

    @pl.when(j == 0)
    def _():
        s_ref[...] = s0_ref[0, 0, 0] if has_s0 else jnp.zeros_like(s_ref)

    h = _modnorm(x_ref[0], gain_ref[...], sc_ref[0], sh_ref[0]).astype(BF)
    proj_ref[0] = _dot(h, w_ref[...])

    for hd in range(H_RET):
        sstart_ref[0, 0, hd] = s_ref[hd].astype(BF)
        k = proj_ref[0, :, K_OFF + hd * DK_RET:K_OFF + (hd + 1) * DK_RET]
        v = proj_ref[0, :, V_OFF + hd * DV_RET:V_OFF + (hd + 1) * DV_RET]
        s_ref[hd] = s_ref[hd] * dec_ref[hd, 0] + _dot_tn((k * vec_ref[hd, ZETA_F]).astype(BF), v.astype(BF))

    if emit_state:
        @pl.when(j == nt - 1)
        def _():
            sfin_ref[0] = s_ref[...]


def _in_proj(x3, gain, mod, w, tables, state):
    b, t, _ = x3.shape
    tm = TOKEN_TILE
    nt = t // tm
    has_s0 = state is not None
    emit_state = not has_s0
    _, vec, dec = tables
    in_specs = [pl.BlockSpec((1, tm, D_MODEL), lambda i, j: (i, j, 0)), _vec_spec(D_MODEL),
                mod.spec_batch(1), mod.spec_batch(0), _resident(w.shape), _resident(vec.shape), _resident(dec.shape)]
    args = [x3, gain, mod.arr, mod.arr, w, vec, dec]
    if has_s0:
        in_specs.append(pl.BlockSpec((1, 1, 1, H_RET, DK_RET, DV_RET), lambda i, j: (i, 0, 0, 0, 0, 0)))
        args.append(state)
    out_shape = [jax.ShapeDtypeStruct((b, t, EVEN_IN), F32),
                 jax.ShapeDtypeStruct((b, nt, H_RET, DK_RET, DV_RET), BF)]
    out_specs = [pl.BlockSpec((1, tm, EVEN_IN), lambda i, j: (i, j, 0)),
                 pl.BlockSpec((1, 1, H_RET, DK_RET, DV_RET), lambda i, j: (i, j, 0, 0, 0))]
    if emit_state:
        out_shape.append(jax.ShapeDtypeStruct((b, H_RET, DK_RET, DV_RET), F32))
        out_specs.append(pl.BlockSpec((1, H_RET, DK_RET, DV_RET), lambda i, j: (i, 0, 0, 0)))
    return pl.pallas_call(
        functools.partial(_in_proj_kernel, nt=nt, has_s0=has_s0, emit_state=emit_state),
        out_shape=out_shape,
        grid=(b, nt),
        in_specs=in_specs,
        out_specs=out_specs,
        scratch_shapes=[pltpu.VMEM((H_RET, DK_RET, DV_RET), F32)],
        compiler_params=_params("parallel", "arbitrary"),
        name="in_proj",
    )(*args)


def _mixer_kernel(x_ref, proj_ref, prev_ref, next_ref, sstart_ref, dmask_ref, vec_ref, dec_ref, gn_ref, pw_ref,
                  ps_ref, wout_ref, gain_ref, gate_ref, *rest, nt, t_len, has_s0, emit_state):
    rest = list(rest)
    s0_ref = rest.pop(0) if has_s0 else None
    sfwd_ref = rest.pop(0) if emit_state else None
    o_ref = rest.pop(0)
    sfin_ref = rest.pop(0) if emit_state else None
    s_ref, pad_ref = rest
    j = pl.program_id(1)
    tile = nt - 1 - j
    c_len = TOKEN_TILE

    @pl.when(j == 0)
    def _():
        s_ref[...] = s0_ref[0, 0, 0] if has_s0 else jnp.zeros_like(s_ref)

    gn = gn_ref[...]
    pieces = []
    for hd in range(H_RET):
        q = proj_ref[0, :, Q_OFF + hd * DK_RET:Q_OFF + (hd + 1) * DK_RET]
        k = proj_ref[0, :, K_OFF + hd * DK_RET:K_OFF + (hd + 1) * DK_RET]
        v = proj_ref[0, :, V_OFF + hd * DV_RET:V_OFF + (hd + 1) * DV_RET].astype(BF)
        g = proj_ref[0, :, G_OFF + hd * DV_RET:G_OFF + (hd + 1) * DV_RET]
        scores = _dot_nt(q.astype(BF), k.astype(BF)) * dmask_ref[hd]
        inner = _dot(scores.astype(BF), v)
        q_both = jnp.concatenate([(q * vec_ref[hd, XI_F]).astype(BF), (q * vec_ref[hd, XI_B]).astype(BF)], axis=1)
        s_both = jnp.concatenate([sstart_ref[0, 0, hd], s_ref[hd].astype(BF)], axis=0)
        y = inner + _dot(q_both, s_both)
        s_ref[hd] = s_ref[hd] * dec_ref[hd, 1] + _dot_tn((k * vec_ref[hd, ZETA_B]).astype(BF), v)

        mu = jnp.mean(y, axis=-1, keepdims=True)
        yc = y - mu
        var = jnp.mean(yc * yc, axis=-1, keepdims=True)
        yn = (yc * lax.rsqrt(var + EPS)) * gn[:, hd * DV_RET:(hd + 1) * DV_RET]
        pieces.append((_silu(g) * yn).astype(BF))

    if emit_state:
        @pl.when(j == nt - 1)
        def _():
            sfin_ref[0, 0] = sfwd_ref[0]
            sfin_ref[0, 1] = s_ref[...]

    pad_ref[0:POOL_PAD, :] = jnp.where(tile > 0, prev_ref[0], 0.0)
    pad_ref[POOL_PAD:POOL_PAD + c_len, :] = proj_ref[0, :, U_OFF:U_OFF + POOL_WIDTH]
    pad_ref[POOL_PAD + c_len:2 * POOL_PAD + c_len, :] = jnp.where(tile < nt - 1, next_ref[0], 0.0)
    pos = tile * c_len + lax.broadcasted_iota(jnp.int32, (c_len, 1), 0)
    for grp, window in enumerate(POOL_WINDOWS):
        half = window // 2
        lanes = slice(grp * POOL_GROUP_DIM, (grp + 1) * POOL_GROUP_DIM)
        acc = pad_ref[POOL_PAD - half:POOL_PAD - half + c_len, lanes]
        for off in range(-half + 1, half):
            acc = acc + pad_ref[POOL_PAD + off:POOL_PAD + off + c_len, lanes]
        cnt = (jnp.minimum(pos + half, t_len) - jnp.maximum(pos - half, 0)).astype(F32)
        d = acc / cnt - pad_ref[POOL_PAD:POOL_PAD + c_len, lanes]
        pieces.append((_dot(d.astype(BF), pw_ref[grp]) * ps_ref[:, lanes]).astype(BF))

    o = _dot(jnp.concatenate(pieces, axis=1), wout_ref[...])
    o_ref[0] = x_ref[0] + gate_ref[0] * (_rms(o) * gain_ref[...])


def _mixer(x3, proj, sstart, mod, tables, gn_gain, pool_w, pool_scale, w_out, gain, state, s_fwd):
    b, t, _ = x3.shape
    tm = TOKEN_TILE
    nt = t // tm
    halo_per_tile = tm // POOL_PAD
    n_halo = t // POOL_PAD
    has_s0 = state is not None
    emit_state = not has_s0
    rev = lambda j: nt - 1 - j
    in_specs = [
        pl.BlockSpec((1, tm, D_MODEL), lambda i, j: (i, rev(j), 0)),
        pl.BlockSpec((1, tm, EVEN_IN), lambda i, j: (i, rev(j), 0)),
        pl.BlockSpec((1, POOL_PAD, POOL_WIDTH),
                     lambda i, j: (i, jnp.maximum(rev(j) * halo_per_tile - 1, 0), U_OFF // POOL_WIDTH)),
        pl.BlockSpec((1, POOL_PAD, POOL_WIDTH),
                     lambda i, j: (i, jnp.minimum((rev(j) + 1) * halo_per_tile, n_halo - 1), U_OFF // POOL_WIDTH)),
        pl.BlockSpec((1, 1, H_RET, DK_RET, DV_RET), lambda i, j: (i, rev(j), 0, 0, 0)),
        *[_resident(a.shape) for a in tables],
        _vec_spec(H_RET * DV_RET), _resident(pool_w.shape), _vec_spec(POOL_WIDTH), _resident(w_out.shape),
        _vec_spec(D_MODEL), mod.spec_batch(2),
    ]
    args = [x3, proj, proj, proj, sstart, *tables, gn_gain, pool_w, pool_scale, w_out, gain, mod.arr]
    out_shape = [jax.ShapeDtypeStruct((b, t, D_MODEL), F32)]
    out_specs = [pl.BlockSpec((1, tm, D_MODEL), lambda i, j: (i, rev(j), 0))]
    if has_s0:
        in_specs.append(pl.BlockSpec((1, 1, 1, H_RET, DK_RET, DV_RET), lambda i, j: (i, 0, 1, 0, 0, 0)))
        args.append(state)
    else:
        in_specs.append(pl.BlockSpec((1, H_RET, DK_RET, DV_RET), lambda i, j: (i, 0, 0, 0)))
        args.append(s_fwd)
        out_shape.append(jax.ShapeDtypeStruct((b, 2, H_RET, DK_RET, DV_RET), F32))
        out_specs.append(pl.BlockSpec((1, 2, H_RET, DK_RET, DV_RET), lambda i, j: (i, 0, 0, 0, 0)))
    outs = pl.pallas_call(
        functools.partial(_mixer_kernel, nt=nt, t_len=t, has_s0=has_s0, emit_state=emit_state),
        out_shape=out_shape,
        grid=(b, nt),
        in_specs=in_specs,
        out_specs=out_specs,
        scratch_shapes=[pltpu.VMEM((H_RET, DK_RET, DV_RET), F32),
                        pltpu.VMEM((tm + 2 * POOL_PAD, POOL_WIDTH), F32)],
        compiler_params=_params("parallel", "arbitrary"),
        name="mixer",
    )(*args)
    return outs if emit_state else (outs[0], None)


def _ffn_kernel(x_ref, gain_in_ref, sc_ref, sh_ref, gate_ref, gain_out_ref, win_ref, wout_ref, o_ref):
    x = x_ref[...]
    h = _modnorm(x, gain_in_ref[...], sc_ref[0], sh_ref[0]).astype(BF)
    f = None
    for start, width in FF_CHUNKS:
        a = _dot(h, win_ref[:, start:start + width])
        b = _dot(h, win_ref[:, D_FF + start:D_FF + start + width])
        part = _dot((_silu(a) * b).astype(BF), wout_ref[start:start + width, :])
        f = part if f is None else f + part
    o_ref[...] = x + gate_ref[0] * (_rms(f) * gain_out_ref[...])


def _ffn(x, gain_in, gain_out, mod, w_in, w_out):
    n_tok = x.shape[0]
    tm = FFN_TILE
    return pl.pallas_call(
        _ffn_kernel,
        out_shape=jax.ShapeDtypeStruct((n_tok, D_MODEL), F32),
        grid=(n_tok // tm,),
        in_specs=[_row_spec(tm, D_MODEL), _vec_spec(D_MODEL), mod.spec(4, tm), mod.spec(3, tm), mod.spec(5, tm),
                  _vec_spec(D_MODEL), _resident(w_in.shape), _resident(w_out.shape)],
        out_specs=_row_spec(tm, D_MODEL),
        compiler_params=_params("parallel"),
        name="ffn",
    )(x, gain_in, mod.arr, mod.arr, mod.arr, gain_out, w_in, w_out)


def _rope(x, cs, sn):
    lane = lax.broadcasted_iota(jnp.int32, x.shape, 1)
    first = (lane % (2 * ROPE_PAIRS)) < ROPE_PAIRS
    partner = jnp.where(first, pltpu.roll(x, HEAD_DIM - ROPE_PAIRS, 1), pltpu.roll(x, ROPE_PAIRS, 1))
    return x * cs + partner * sn


Q_PRESCALE = HEAD_DIM ** -0.5 * 1.4426950408889634


def _qkv_kernel(x_ref, gain_ref, sc_ref, sh_ref, w_ref, qg_ref, kg_ref, *rest, rope, emit_cache):
    rest = list(rest)
    cs_ref = rest.pop(0) if rope else None
    sn_ref = rest.pop(0) if rope else None
    q_ref, kb_ref, vb_ref = rest[:3]
    kf_ref, vf_ref = rest[3:] if emit_cache else (None, None)
    h = _modnorm(x_ref[...], gain_ref[...], sc_ref[0], sh_ref[0]).astype(BF)
    qkv = _dot(h, w_ref[...])

    def head(idx, g_ref):
        xh = _rms(qkv[:, idx * HEAD_DIM:(idx + 1) * HEAD_DIM]) * g_ref[...]
        return _rope(xh, cs_ref[...], sn_ref[...]) if rope else xh

    for i in range(N_HEADS):
        q_ref[:, i * HEAD_DIM:(i + 1) * HEAD_DIM] = (head(i, qg_ref) * Q_PRESCALE).astype(BF)
    for i in range(KV_HEADS):
        lanes = slice(i * HEAD_DIM, (i + 1) * HEAD_DIM)
        kh = head(N_HEADS + i, kg_ref)
        kb_ref[:, lanes] = kh.astype(BF)
        if emit_cache:
            kf_ref[:, lanes] = kh
    v = qkv[:, (N_HEADS + KV_HEADS) * HEAD_DIM:]
    vb_ref[...] = v.astype(BF)
    if emit_cache:
        vf_ref[...] = v


def _qkv(x, gain, mod, w, q_gain, k_gain, rope_tables, emit_cache):
    n_tok = x.shape[0]
    tm = TOKEN_TILE
    rope = rope_tables is not None
    in_specs = [_row_spec(tm, D_MODEL), _vec_spec(D_MODEL), mod.spec(1, tm), mod.spec(0, tm), _resident(w.shape),
                _vec_spec(HEAD_DIM), _vec_spec(HEAD_DIM)]
    args = [x, gain, mod.arr, mod.arr, w, q_gain, k_gain]
    if rope:
        tiles_per_seq = rope_tables[0].shape[0] // tm
        in_specs += [pl.BlockSpec((tm, HEAD_DIM), lambda i: (i % tiles_per_seq, 0))] * 2
        args += list(rope_tables)
    kv_w = KV_HEADS * HEAD_DIM
    widths = [(D_MODEL, BF), (kv_w, BF), (kv_w, BF)] + ([(kv_w, F32), (kv_w, F32)] if emit_cache else [])
    return pl.pallas_call(
        functools.partial(_qkv_kernel, rope=rope, emit_cache=emit_cache),
        out_shape=[jax.ShapeDtypeStruct((n_tok, wd), dt) for wd, dt in widths],
        grid=(n_tok // tm,),
        in_specs=in_specs,
        out_specs=[_row_spec(tm, wd) for wd, _ in widths],
        compiler_params=_params("parallel"),
        name="qkv_proj",
    )(*args)


def _rope_tables(t):
    pos = jnp.arange(t, dtype=jnp.int32)
    row = (pos // GRID_W).astype(F32)
    col = (pos % GRID_W).astype(F32)
    freqs = ROPE_BASE ** (-jnp.arange(ROPE_PAIRS, dtype=F32) / ROPE_PAIRS)
    ang_r = row[:, None] * freqs[None, :]
    ang_c = col[:, None] * freqs[None, :]
    cs = jnp.concatenate([jnp.cos(ang_r)] * 2 + [jnp.cos(ang_c)] * 2, axis=-1)
    sn = jnp.concatenate([-jnp.sin(ang_r), jnp.sin(ang_r), -jnp.sin(ang_c), jnp.sin(ang_c)], axis=-1)
    return cs, sn


KV_BLOCK = 512
LANES = 128


def _fold_lanes(x, op):
    out = x[:, :LANES]
    for i in range(1, x.shape[1] // LANES):
        out = op(out, x[:, i * LANES:(i + 1) * LANES])
    return out


def _attn_kernel(x_ref, q_ref, *rest, seg_lens, blk):
    n_seg = len(seg_lens)
    kv_refs = rest[:2 * n_seg]
    w_ref, gain_ref, gate_ref, o_ref, s_ref, m_ref, acc_ref = rest[2 * n_seg:]
    tq = q_ref.shape[1]
    kv_lanes = [slice(kvh * HEAD_DIM, (kvh + 1) * HEAD_DIM) for kvh in range(KV_HEADS)]

    def over_blocks(body):
        base = 0
        for seg, n in enumerate(seg_lens):
            nb = n // blk
            if nb == 1:
                body(seg, 0, base)
            else:
                def step(i, carry, seg=seg, base=base):
                    body(seg, pl.multiple_of(i * blk, blk), base + i)
                    return carry

                lax.fori_loop(0, nb, step, 0, unroll=True)
            base += nb

    m_ref[...] = jnp.full(m_ref.shape, -jnp.inf, F32)

    def scores(seg, row0, b):
        for kvh in range(KV_HEADS):
            q4 = jnp.concatenate([q_ref[0, :, (kvh * GQA + g) * HEAD_DIM:(kvh * GQA + g + 1) * HEAD_DIM]
                                  for g in range(GQA)], axis=0)
            k = kv_refs[2 * seg][0, pl.ds(row0, blk), kv_lanes[kvh]].astype(BF)
            s = _dot_nt(q4, k)
            s_ref[kvh, b] = s
            m_ref[kvh] = jnp.maximum(m_ref[kvh], _fold_lanes(s, jnp.maximum))

    over_blocks(scores)
    for kvh in range(KV_HEADS):
        m_ref[kvh] = jnp.broadcast_to(jnp.max(m_ref[kvh], axis=-1, keepdims=True), m_ref.shape[1:])

    acc_ref[...] = jnp.zeros(acc_ref.shape, F32)
    ones = jnp.ones((blk, LANES), BF)

    def values(seg, row0, b):
        for kvh in range(KV_HEADS):
            v = kv_refs[2 * seg + 1][0, pl.ds(row0, blk), kv_lanes[kvh]].astype(BF)
            row_max = m_ref[kvh]
            e = jnp.concatenate(
                [jnp.exp2(s_ref[kvh, b, :, t * LANES:(t + 1) * LANES] - row_max).astype(BF)
                 for t in range(blk // LANES)], axis=1)
            acc_ref[kvh] += _dot(e, jnp.concatenate([v, ones], axis=1))

    over_blocks(values)
    pieces = []
    for kvh in range(KV_HEADS):
        att = acc_ref[kvh, :, :HEAD_DIM] / acc_ref[kvh, :, HEAD_DIM:]
        pieces += [att[g * tq:(g + 1) * tq].astype(BF) for g in range(GQA)]

    o = _dot(jnp.concatenate(pieces, axis=1), w_ref[...])
    o_ref[0] = x_ref[0] + gate_ref[0] * (_rms(o) * gain_ref[...])


def _attention(x3, q3, kv_segments, w_o, gain, mod):
    b, t, _ = x3.shape
    tq = TOKEN_TILE
    seg_lens = tuple(k.shape[1] for k, _ in kv_segments)
    blk = min(KV_BLOCK, *seg_lens)
    assert all(n % blk == 0 for n in seg_lens)
    n_blocks = sum(seg_lens) // blk
    rows = GQA * tq
    kv_w = KV_HEADS * HEAD_DIM
    in_specs = [pl.BlockSpec((1, tq, D_MODEL), lambda i, j: (i, j, 0)),
                pl.BlockSpec((1, tq, D_MODEL), lambda i, j: (i, j, 0))]
    args = [x3, q3]
    for k, v in kv_segments:
        in_specs += [pl.BlockSpec((1, k.shape[1], kv_w), lambda i, j: (i, 0, 0))] * 2
        args += [k, v]
    in_specs += [_resident(w_o.shape), _vec_spec(D_MODEL), mod.spec_batch(2)]
    args += [w_o, gain, mod.arr]
    return pl.pallas_call(
        functools.partial(_attn_kernel, seg_lens=seg_lens, blk=blk),
        out_shape=jax.ShapeDtypeStruct(x3.shape, F32),
        grid=(b, t // tq),
        in_specs=in_specs,
        out_specs=pl.BlockSpec((1, tq, D_MODEL), lambda i, j: (i, j, 0)),
        scratch_shapes=[pltpu.VMEM((KV_HEADS, n_blocks, rows, blk), F32),
                        pltpu.VMEM((KV_HEADS, rows, LANES), F32),
                        pltpu.VMEM((KV_HEADS, rows, HEAD_DIM + LANES), F32)],
        compiler_params=_params("parallel", "parallel"),
        name="attention",
    )(*args)


def _even_layer(x3, mod, norm_gain, w_in, tables, gn_gain, pool_w, pool_scale, w_out, state):
    proj, sstart, *s_fwd = _in_proj(x3, norm_gain[0:1], mod, w_in, tables, state)
    x3, s_fin = _mixer(x3, proj, sstart, mod, tables, gn_gain, pool_w, pool_scale, w_out, norm_gain[1:2], state,
                       s_fwd[0] if s_fwd else None)
    return x3.reshape(-1, D_MODEL), s_fin


def _odd_layer(x, b, mod, norm_gain, w_qkv, q_gain, k_gain, w_o, cache, rope_tables):
    t = x.shape[0] // b
    emit_cache = cache is None
    q, kb, vb, *kv_f32 = _qkv(x, norm_gain[0:1], mod, w_qkv, q_gain, k_gain, rope_tables, emit_cache)
    as3 = lambda a: a.reshape(b, t, -1)
    segments = [(as3(kb), as3(vb))]
    if cache is not None:
        segments.insert(0, tuple(a.reshape(b, -1, KV_HEADS * HEAD_DIM) for a in cache))
    x3 = _attention(as3(x), as3(q), segments, w_o, norm_gain[1:2], mod)
    return x3.reshape(-1, D_MODEL), kv_f32


def kernel(x_prompt, x_sample, state_ret, cache_k, cache_v, c, c_ctx, w_ada, b_ada, norm_gain, w_ffn_in, w_ffn_out,
           w_in_even, ret_decay_logit, ret_gn_gain, pool_w, pool_scale, w_out_even, w_qkv, q_norm_gain, k_norm_gain,
           w_o):
    bp, tp, _ = x_prompt.shape
    bs, ts, _ = x_sample.shape
    depth = w_ada.shape[0]
    assert bs + 1 <= MOD_ROWS and depth == 2

    cvec = jnp.zeros((MOD_ROWS, D_MODEL), F32).at[0].set(c_ctx).at[1:1 + bs].set(c)
    mod_flat = _modulation(cvec, w_ada, b_ada).reshape(depth * MOD_ROWS * MOD_PARTS, 1, D_MODEL)
    bf = lambda w: w.astype(BF)

    mod_p = _Mod(mod_flat, 0, None)
    mod_s = _Mod(mod_flat, 0, ts)
    tables = _decay_tables(jax.nn.log_sigmoid(ret_decay_logit[0].astype(F32)))
    even = functools.partial(_even_layer, norm_gain=norm_gain[0], w_in=bf(w_in_even[0]), tables=tables,
                             gn_gain=ret_gn_gain[0:1], pool_w=bf(pool_w[0]), pool_scale=pool_scale[0:1],
                             w_out=bf(w_out_even[0]))
    xp, s_fin = even(x_prompt, mod_p, state=None)
    xs, _ = even(x_sample, mod_s, state=state_ret)
    ffn0 = functools.partial(_ffn, gain_in=norm_gain[0, 2:3], gain_out=norm_gain[0, 3:4], w_in=bf(w_ffn_in[0]),
                             w_out=bf(w_ffn_out[0]))
    xp = ffn0(xp, mod=mod_p)
    xs = ffn0(xs, mod=mod_s)

    mod_p = _Mod(mod_flat, 1, None)
    mod_s = _Mod(mod_flat, 1, ts)
    odd = functools.partial(_odd_layer, norm_gain=norm_gain[1], w_qkv=bf(w_qkv[0]), q_gain=q_norm_gain[0:1],
                            k_gain=k_norm_gain[0:1], w_o=bf(w_o[0]))
    xp, (kp, vp) = odd(xp, bp, mod_p, cache=None, rope_tables=None)
    xs, _ = odd(xs, bs, mod_s, cache=(cache_k[:, 0], cache_v[:, 0]), rope_tables=_rope_tables(ts))
    ffn1 = functools.partial(_ffn, gain_in=norm_gain[1, 2:3], gain_out=norm_gain[1, 3:4], w_in=bf(w_ffn_in[1]),
                             w_out=bf(w_ffn_out[1]))
    xp = ffn1(xp, mod=mod_p)
    xs = ffn1(xs, mod=mod_s)

    new_state_ret = s_fin.reshape(bp, 1, 2, H_RET, DK_RET, DV_RET)
    new_cache_k = kp.reshape(bp, 1, tp, KV_HEADS, HEAD_DIM)
    new_cache_v = vp.reshape(bp, 1, tp, KV_HEADS, HEAD_DIM)
    return (xp.reshape(bp, tp, D_MODEL), xs.reshape(bs, ts, D_MODEL), new_state_ret, new_cache_k, new_cache_v)
```

```python
import functools

import jax
import jax.numpy as jnp
from jax import lax
from jax.experimental import pallas as pl
from jax.experimental.pallas import tpu as pltpu

D_MODEL = 1024
EPS = 1e-6
GRID_W = 64
H_RET = 4
DK_RET = 128
DV_RET = 256
POOL_WINDOWS = (2, 4, 8, 16)
POOL_GROUP_DIM = 128
POOL_WIDTH = 512
POOL_PAD = 8
HALO_BLOCK = 16
Q_OFF, K_OFF, V_OFF, G_OFF, U_OFF = 0, 512, 1024, 2048, 3072
EVEN_IN = 3584
HEAD_DIM = 128
N_HEADS = 8
KV_HEADS = 2
GQA = N_HEADS // KV_HEADS
ROPE_BASE = 10000.0
ROPE_PAIRS = 32
QKV_OUT = (N_HEADS + 2 * KV_HEADS) * HEAD_DIM
D_FF = 2816
FF_CHUNKS = ((0, 1536), (1536, 1280))
MOD_ROWS = 8
MOD_PARTS = 6

BF = jnp.bfloat16
F32 = jnp.float32
VMEM_LIMIT_BYTES = 52 * 1024 * 1024
TOKEN_TILE = 256
FFN_TILE = 512


def _params(*sem):
    return pltpu.CompilerParams(dimension_semantics=sem, vmem_limit_bytes=VMEM_LIMIT_BYTES)


def _resident(shape):
    nd = len(shape)
    return pl.BlockSpec(shape, lambda *_: (0,) * nd, pipeline_mode=pl.Buffered(1))


def _rms(x):
    return x * lax.rsqrt(jnp.mean(x * x, axis=-1, keepdims=True) + EPS)


def _modnorm(x, gain, sc, sh):
    return (_rms(x) * gain) * (1.0 + sc) + sh


def _silu(x):
    return x * jax.nn.sigmoid(x)


def _dot(a, b):
    return jnp.dot(a, b, preferred_element_type=F32)


def _dot_nt(a, b):
    return lax.dot_general(a, b, (((1,), (1,)), ((), ())), preferred_element_type=F32)


def _dot_tn(a, b):
    return lax.dot_general(a, b, (((0,), (0,)), ((), ())), preferred_element_type=F32)


def _mod_kernel(c_ref, w_ref, b_ref, o_ref):
    s = _silu(c_ref[...]).astype(BF)
    o_ref[0] = _dot(s, w_ref[0].astype(BF)) + b_ref[0]


def _modulation(cvec, w_ada, b_ada):
    depth = w_ada.shape[0]
    n = w_ada.shape[2]
    tn = 1536
    return pl.pallas_call(
        _mod_kernel,
        out_shape=jax.ShapeDtypeStruct((depth, MOD_ROWS, n), F32),
        grid=(depth, n // tn),
        in_specs=[
            pl.BlockSpec((MOD_ROWS, D_MODEL), lambda l, j: (0, 0)),
            pl.BlockSpec((1, D_MODEL, tn), lambda l, j: (l, 0, j)),
            pl.BlockSpec((1, 1, tn), lambda l, j: (l, 0, j)),
        ],
        out_specs=pl.BlockSpec((1, MOD_ROWS, tn), lambda l, j: (l, 0, j)),
        compiler_params=_params("parallel", "parallel"),
        name="adaln_mod",
    )(cvec, w_ada, b_ada.reshape(depth, 1, n))


class _Mod:
    def __init__(self, mod_flat, layer, tokens_per_row):
        self.arr = mod_flat
        self.layer = layer
        self.tokens_per_row = tokens_per_row

    def _block(self, row, part):
        return ((self.layer * MOD_ROWS + row) * MOD_PARTS + part, 0, 0)

    def spec(self, part, tile):
        def index(i, *_):
            return self._block(0 if self.tokens_per_row is None else 1 + (i * tile) // self.tokens_per_row, part)

        return pl.BlockSpec((1, 1, D_MODEL), index)

    def spec_batch(self, part):
        def index(b, *_):
            return self._block(0 if self.tokens_per_row is None else 1 + b, part)

        return pl.BlockSpec((1, 1, D_MODEL), index)


def _row_spec(tile, width):
    return pl.BlockSpec((tile, width), lambda i: (i, 0))


def _vec_spec(width):
    return pl.BlockSpec((1, width), lambda *_: (0, 0))


def _positions(n):
    return lax.broadcasted_iota(jnp.int32, (n, 1), 0).astype(F32)


XI_F, XI_B, ZETA_F, ZETA_B = range(4)


def _decay_kernel(lg_ref, dmask_ref, vec_ref, dec_ref):
    hd = pl.program_id(0)
    lgf = lg_ref[0, hd]
    lgb = lg_ref[1, hd]
    c_len = TOKEN_TILE
    qscale = DK_RET ** -0.5
    diff = _positions(c_len) - lax.broadcasted_iota(jnp.int32, (1, c_len), 1).astype(F32)
    dmask_ref[0] = qscale * (jnp.where(diff >= 0, jnp.exp(jnp.maximum(diff, 0.0) * lgf), 0.0)
                             + jnp.where(diff <= 0, jnp.exp(jnp.maximum(-diff, 0.0) * lgb), 0.0))
    n = lax.broadcasted_iota(jnp.int32, (c_len, DK_RET), 0).astype(F32)
    vec_ref[0, XI_F] = qscale * jnp.exp((n + 1.0) * lgf)
    vec_ref[0, XI_B] = qscale * jnp.exp((c_len - n) * lgb)
    vec_ref[0, ZETA_F] = jnp.exp((c_len - 1.0 - n) * lgf)
    vec_ref[0, ZETA_B] = jnp.exp(n * lgb)
    chunk = jnp.full((1, DV_RET), c_len, F32)
    dec_ref[0, 0] = jnp.exp(chunk * lgf)
    dec_ref[0, 1] = jnp.exp(chunk * lgb)


def _decay_tables(lg):
    c_len = TOKEN_TILE
    return pl.pallas_call(
        _decay_kernel,
        out_shape=[jax.ShapeDtypeStruct((H_RET, c_len, c_len), F32),
                   jax.ShapeDtypeStruct((H_RET, 4, c_len, DK_RET), F32),
                   jax.ShapeDtypeStruct((H_RET, 2, 1, DV_RET), F32)],
        grid_spec=pltpu.PrefetchScalarGridSpec(
            num_scalar_prefetch=1, grid=(H_RET,), in_specs=[],
            out_specs=[pl.BlockSpec((1, c_len, c_len), lambda h, lg: (h, 0, 0)),
                       pl.BlockSpec((1, 4, c_len, DK_RET), lambda h, lg: (h, 0, 0, 0)),
                       pl.BlockSpec((1, 2, 1, DV_RET), lambda h, lg: (h, 0, 0, 0))]),
        compiler_params=_params("arbitrary"),
        name="decay_tables",
    )(lg)


def _in_proj_kernel(x_ref, gain_ref, sc_ref, sh_ref, w_ref, vec_ref, dec_ref, *rest, nt, has_s0, emit_state):
    rest = list(rest)
    s0_ref = rest.pop(0) if has_s0 else None
    proj_ref = rest.pop(0)
    sstart_ref = rest.pop(0)
    sfin_ref = rest.pop(0) if emit_state else None
    (s_ref,) = rest
    j = pl.program_id(1)

    @pl.when(j == 0)
    def _():
        s_ref[...] = s0_ref[0, 0, 0] if has_s0 else jnp.zeros_like(s_ref)

    h = _modnorm(x_ref[0], gain_ref[...], sc_ref[0], sh_ref[0]).astype(BF)
    proj = _dot(h, w_ref[...])
    proj_ref[0] = proj.astype(BF)

    for hd in range(H_RET):
        sstart_ref[0, 0, hd] = s_ref[hd].astype(BF)
        k = proj[:, K_OFF + hd * DK_RET:K_OFF + (hd + 1) * DK_RET]
        v = proj[:, V_OFF + hd * DV_RET:V_OFF + (hd + 1) * DV_RET]
        s_ref[hd] = s_ref[hd] * dec_ref[hd, 0] + _dot_tn((k * vec_ref[hd, ZETA_F]).astype(BF), v.astype(BF))

    if emit_state:
        @pl.when(j == nt - 1)
        def _():
            sfin_ref[0] = s_ref[...]


def _in_proj(x3, gain, mod, w, tables, state):
    b, t, _ = x3.shape
    tm = TOKEN_TILE
    nt = t // tm
    has_s0 = state is not None
    emit_state = not has_s0
    _, vec, dec = tables
    in_specs = [pl.BlockSpec((1, tm, D_MODEL), lambda i, j: (i, j, 0)), _vec_spec(D_MODEL),
                mod.spec_batch(1), mod.spec_batch(0), _resident(w.shape), _resident(vec.shape), _resident(dec.shape)]
    args = [x3, gain, mod.arr, mod.arr, w, vec, dec]
    if has_s0:
        in_specs.append(pl.BlockSpec((1, 1, 1, H_RET, DK_RET, DV_RET), lambda i, j: (i, 0, 0, 0, 0, 0)))
        args.append(state)
    out_shape = [jax.ShapeDtypeStruct((b, t, EVEN_IN), BF),
                 jax.ShapeDtypeStruct((b, nt, H_RET, DK_RET, DV_RET), BF)]
    out_specs = [pl.BlockSpec((1, tm, EVEN_IN), lambda i, j: (i, j, 0)),
                 pl.BlockSpec((1, 1, H_RET, DK_RET, DV_RET), lambda i, j: (i, j, 0, 0, 0))]
    if emit_state:
        out_shape.append(jax.ShapeDtypeStruct((b, H_RET, DK_RET, DV_RET), F32))
        out_specs.append(pl.BlockSpec((1, H_RET, DK_RET, DV_RET), lambda i, j: (i, 0, 0, 0)))
    return pl.pallas_call(
        functools.partial(_in_proj_kernel, nt=nt, has_s0=has_s0, emit_state=emit_state),
        out_shape=out_shape,
        grid=(b, nt),
        in_specs=in_specs,
        out_specs=out_specs,
        scratch_shapes=[pltpu.VMEM((H_RET, DK_RET, DV_RET), F32)],
        compiler_params=_params("parallel", "arbitrary"),
        name="in_proj",
    )(*args)


def _mixer_kernel(x_ref, proj_ref, prev_ref, next_ref, sstart_ref, dmask_ref, vec_ref, dec_ref, gn_ref, pw_ref,
                  ps_ref, wout_ref, gain_ref, gate_ref, *rest, nt, t_len, has_s0, emit_state):
    rest = list(rest)
    s0_ref = rest.pop(0) if has_s0 else None
    sfwd_ref = rest.pop(0) if emit_state else None
    o_ref = rest.pop(0)
    sfin_ref = rest.pop(0) if emit_state else None
    s_ref, pad_ref = rest
    j = pl.program_id(1)
    tile = nt - 1 - j
    c_len = TOKEN_TILE

    @pl.when(j == 0)
    def _():
        s_ref[...] = s0_ref[0, 0, 0] if has_s0 else jnp.zeros_like(s_ref)

    gn = gn_ref[...]
    pieces = []
    for hd in range(H_RET):
        q = proj_ref[0, :, Q_OFF + hd * DK_RET:Q_OFF + (hd + 1) * DK_RET]
        k = proj_ref[0, :, K_OFF + hd * DK_RET:K_OFF + (hd + 1) * DK_RET]
        v = proj_ref[0, :, V_OFF + hd * DV_RET:V_OFF + (hd + 1) * DV_RET]
        g = proj_ref[0, :, G_OFF + hd * DV_RET:G_OFF + (hd + 1) * DV_RET].astype(F32)
        scores = _dot_nt(q, k) * dmask_ref[hd]
        inner = _dot(scores.astype(BF), v)
        q32 = q.astype(F32)
        q_both = jnp.concatenate([(q32 * vec_ref[hd, XI_F]).astype(BF), (q32 * vec_ref[hd, XI_B]).astype(BF)],
                                 axis=1)
        s_both = jnp.concatenate([sstart_ref[0, 0, hd], s_ref[hd].astype(BF)], axis=0)
        y = inner + _dot(q_both, s_both)
        s_ref[hd] = s_ref[hd] * dec_ref[hd, 1] + _dot_tn((k.astype(F32) * vec_ref[hd, ZETA_B]).astype(BF), v)

        mu = jnp.mean(y, axis=-1, keepdims=True)
        yc = y - mu
        var = jnp.mean(yc * yc, axis=-1, keepdims=True)
        yn = (yc * lax.rsqrt(var + EPS)) * gn[:, hd * DV_RET:(hd + 1) * DV_RET]
        pieces.append((_silu(g) * yn).astype(BF))

    if emit_state:
        @pl.when(j == nt - 1)
        def _():
            sfin_ref[0, 0] = sfwd_ref[0]
            sfin_ref[0, 1] = s_ref[...]

    prev_rows = prev_ref[0].astype(F32)[HALO_BLOCK - POOL_PAD:]
    next_rows = next_ref[0].astype(F32)[:POOL_PAD]
    pad_ref[0:POOL_PAD, :] = jnp.where(tile > 0, prev_rows, 0.0)
    pad_ref[POOL_PAD:POOL_PAD + c_len, :] = proj_ref[0, :, U_OFF:U_OFF + POOL_WIDTH].astype(F32)
    pad_ref[POOL_PAD + c_len:2 * POOL_PAD + c_len, :] = jnp.where(tile < nt - 1, next_rows, 0.0)
    pos = tile * c_len + lax.broadcasted_iota(jnp.int32, (c_len, 1), 0)
    for grp, window in enumerate(POOL_WINDOWS):
        half = window // 2
        lanes = slice(grp * POOL_GROUP_DIM, (grp + 1) * POOL_GROUP_DIM)
        acc = pad_ref[POOL_PAD - half:POOL_PAD - half + c_len, lanes]
        for off in range(-half + 1, half):
            acc = acc + pad_ref[POOL_PAD + off:POOL_PAD + off + c_len, lanes]
        cnt = (jnp.minimum(pos + half, t_len) - jnp.maximum(pos - half, 0)).astype(F32)
        d = acc / cnt - pad_ref[POOL_PAD:POOL_PAD + c_len, lanes]
        pieces.append((_dot(d.astype(BF), pw_ref[grp]) * ps_ref[:, lanes]).astype(BF))

    o = _dot(jnp.concatenate(pieces, axis=1), wout_ref[...])
    o_ref[0] = x_ref[0] + gate_ref[0] * (_rms(o) * gain_ref[...])


def _mixer(x3, proj, sstart, mod, tables, gn_gain, pool_w, pool_scale, w_out, gain, state, s_fwd):
    b, t, _ = x3.shape
    tm = TOKEN_TILE
    nt = t // tm
    halo_per_tile = tm // HALO_BLOCK
    n_halo = t // HALO_BLOCK
    has_s0 = state is not None
    emit_state = not has_s0
    rev = lambda j: nt - 1 - j
    in_specs = [
        pl.BlockSpec((1, tm, D_MODEL), lambda i, j: (i, rev(j), 0)),
        pl.BlockSpec((1, tm, EVEN_IN), lambda i, j: (i, rev(j), 0)),
        pl.BlockSpec((1, HALO_BLOCK, POOL_WIDTH),
                     lambda i, j: (i, jnp.maximum(rev(j) * halo_per_tile - 1, 0), U_OFF // POOL_WIDTH)),
        pl.BlockSpec((1, HALO_BLOCK, POOL_WIDTH),
                     lambda i, j: (i, jnp.minimum((rev(j) + 1) * halo_per_tile, n_halo - 1), U_OFF // POOL_WIDTH)),
        pl.BlockSpec((1, 1, H_RET, DK_RET, DV_RET), lambda i, j: (i, rev(j), 0, 0, 0)),
        *[_resident(a.shape) for a in tables],
        _vec_spec(H_RET * DV_RET), _resident(pool_w.shape), _vec_spec(POOL_WIDTH), _resident(w_out.shape),
        _vec_spec(D_MODEL), mod.spec_batch(2),
    ]
    args = [x3, proj, proj, proj, sstart, *tables, gn_gain, pool_w, pool_scale, w_out, gain, mod.arr]
    out_shape = [jax.ShapeDtypeStruct((b, t, D_MODEL), F32)]
    out_specs = [pl.BlockSpec((1, tm, D_MODEL), lambda i, j: (i, rev(j), 0))]
    if has_s0:
        in_specs.append(pl.BlockSpec((1, 1, 1, H_RET, DK_RET, DV_RET), lambda i, j: (i, 0, 1, 0, 0, 0)))
        args.append(state)
    else:
        in_specs.append(pl.BlockSpec((1, H_RET, DK_RET, DV_RET), lambda i, j: (i, 0, 0, 0)))
        args.append(s_fwd)
        out_shape.append(jax.ShapeDtypeStruct((b, 2, H_RET, DK_RET, DV_RET), F32))
        out_specs.append(pl.BlockSpec((1, 2, H_RET, DK_RET, DV_RET), lambda i, j: (i, 0, 0, 0, 0)))
    outs = pl.pallas_call(
        functools.partial(_mixer_kernel, nt=nt, t_len=t, has_s0=has_s0, emit_state=emit_state),
        out_shape=out_shape,
        grid=(b, nt),
        in_specs=in_specs,
        out_specs=out_specs,
        scratch_shapes=[pltpu.VMEM((H_RET, DK_RET, DV_RET), F32),
                        pltpu.VMEM((tm + 2 * POOL_PAD, POOL_WIDTH), F32)],
        compiler_params=_params("parallel", "arbitrary"),
        name="mixer",
    )(*args)
    return outs if emit_state else (outs[0], None)


def _ffn_kernel(x_ref, gain_in_ref, sc_ref, sh_ref, gate_ref, gain_out_ref, win_ref, wout_ref, o_ref):
    x = x_ref[...]
    h = _modnorm(x, gain_in_ref[...], sc_ref[0], sh_ref[0]).astype(BF)
    f = None
    for start, width in FF_CHUNKS:
        a = _dot(h, win_ref[:, start:start + width])
        b = _dot(h, win_ref[:, D_FF + start:D_FF + start + width])
        part = _dot((_silu(a) * b).astype(BF), wout_ref[start:start + width, :])
        f = part if f is None else f + part
    o_ref[...] = x + gate_ref[0] * (_rms(f) * gain_out_ref[...])


def _ffn(x, gain_in, gain_out, mod, w_in, w_out):
    n_tok = x.shape[0]
    tm = FFN_TILE
    return pl.pallas_call(
        _ffn_kernel,
        out_shape=jax.ShapeDtypeStruct((n_tok, D_MODEL), F32),
        grid=(n_tok // tm,),
        in_specs=[_row_spec(tm, D_MODEL), _vec_spec(D_MODEL), mod.spec(4, tm), mod.spec(3, tm), mod.spec(5, tm),
                  _vec_spec(D_MODEL), _resident(w_in.shape), _resident(w_out.shape)],
        out_specs=_row_spec(tm, D_MODEL),
        compiler_params=_params("parallel"),
        name="ffn",
    )(x, gain_in, mod.arr, mod.arr, mod.arr, gain_out, w_in, w_out)


def _rope(x, cs, sn):
    lane = lax.broadcasted_iota(jnp.int32, x.shape, 1)
    first = (lane % (2 * ROPE_PAIRS)) < ROPE_PAIRS
    partner = jnp.where(first, pltpu.roll(x, HEAD_DIM - ROPE_PAIRS, 1), pltpu.roll(x, ROPE_PAIRS, 1))
    return x * cs + partner * sn


Q_PRESCALE = HEAD_DIM ** -0.5 * 1.4426950408889634


def _qkv_kernel(x_ref, gain_ref, sc_ref, sh_ref, w_ref, qg_ref, kg_ref, *rest, rope, emit_cache):
    rest = list(rest)
    cs_ref = rest.pop(0) if rope else None
    sn_ref = rest.pop(0) if rope else None
    q_ref, kb_ref, vb_ref = rest[:3]
    kf_ref, vf_ref = rest[3:] if emit_cache else (None, None)
    h = _modnorm(x_ref[...], gain_ref[...], sc_ref[0], sh_ref[0]).astype(BF)
    qkv = _dot(h, w_ref[...])

    def head(idx, g_ref):
        xh = _rms(qkv[:, idx * HEAD_DIM:(idx + 1) * HEAD_DIM]) * g_ref[...]
        return _rope(xh, cs_ref[...], sn_ref[...]) if rope else xh

    for i in range(N_HEADS):
        q_ref[:, i * HEAD_DIM:(i + 1) * HEAD_DIM] = (head(i, qg_ref) * Q_PRESCALE).astype(BF)
    for i in range(KV_HEADS):
        lanes = slice(i * HEAD_DIM, (i + 1) * HEAD_DIM)
        kh = head(N_HEADS + i, kg_ref)
        kb_ref[:, lanes] = kh.astype(BF)
        if emit_cache:
            kf_ref[:, lanes] = kh
    v = qkv[:, (N_HEADS + KV_HEADS) * HEAD_DIM:]
    vb_ref[...] = v.astype(BF)
    if emit_cache:
        vf_ref[...] = v


def _qkv(x, gain, mod, w, q_gain, k_gain, rope_tables, emit_cache):
    n_tok = x.shape[0]
    tm = TOKEN_TILE
    rope = rope_tables is not None
    in_specs = [_row_spec(tm, D_MODEL), _vec_spec(D_MODEL), mod.spec(1, tm), mod.spec(0, tm), _resident(w.shape),
                _vec_spec(HEAD_DIM), _vec_spec(HEAD_DIM)]
    args = [x, gain, mod.arr, mod.arr, w, q_gain, k_gain]
    if rope:
        tiles_per_seq = rope_tables[0].shape[0] // tm
        in_specs += [pl.BlockSpec((tm, HEAD_DIM), lambda i: (i % tiles_per_seq, 0))] * 2
        args += list(rope_tables)
    kv_w = KV_HEADS * HEAD_DIM
    widths = [(D_MODEL, BF), (kv_w, BF), (kv_w, BF)] + ([(kv_w, F32), (kv_w, F32)] if emit_cache else [])
    return pl.pallas_call(
        functools.partial(_qkv_kernel, rope=rope, emit_cache=emit_cache),
        out_shape=[jax.ShapeDtypeStruct((n_tok, wd), dt) for wd, dt in widths],
        grid=(n_tok // tm,),
        in_specs=in_specs,
        out_specs=[_row_spec(tm, wd) for wd, _ in widths],
        compiler_params=_params("parallel"),
        name="qkv_proj",
    )(*args)


def _rope_tables(t):
    pos = jnp.arange(t, dtype=jnp.int32)
    row = (pos // GRID_W).astype(F32)
    col = (pos % GRID_W).astype(F32)
    freqs = ROPE_BASE ** (-jnp.arange(ROPE_PAIRS, dtype=F32) / ROPE_PAIRS)
    ang_r = row[:, None] * freqs[None, :]
    ang_c = col[:, None] * freqs[None, :]
    cs = jnp.concatenate([jnp.cos(ang_r)] * 2 + [jnp.cos(ang_c)] * 2, axis=-1)
    sn = jnp.concatenate([-jnp.sin(ang_r), jnp.sin(ang_r), -jnp.sin(ang_c), jnp.sin(ang_c)], axis=-1)
    return cs, sn


KV_BLOCK = 512
LANES = 128


def _fold_lanes(x, op):
    out = x[:, :LANES]
    for i in range(1, x.shape[1] // LANES):
        out = op(out, x[:, i * LANES:(i + 1) * LANES])
    return out


def _attn_kernel(x_ref, q_ref, *rest, seg_lens, blk):
    n_seg = len(seg_lens)
    kv_refs = rest[:2 * n_seg]
    w_ref, gain_ref, gate_ref, o_ref, s_ref, m_ref, acc_ref = rest[2 * n_seg:]
    tq = q_ref.shape[1]
    kv_lanes = [slice(kvh * HEAD_DIM, (kvh + 1) * HEAD_DIM) for kvh in range(KV_HEADS)]

    def over_blocks(body):
        base = 0
        for seg, n in enumerate(seg_lens):
            nb = n // blk
            if nb == 1:
                body(seg, 0, base)
            else:
                def step(i, carry, seg=seg, base=base):
                    body(seg, pl.multiple_of(i * blk, blk), base + i)
                    return carry

                lax.fori_loop(0, nb, step, 0, unroll=True)
            base += nb

    m_ref[...] = jnp.full(m_ref.shape, -jnp.inf, F32)

    def scores(seg, row0, b):
        for kvh in range(KV_HEADS):
            q4 = jnp.concatenate([q_ref[0, :, (kvh * GQA + g) * HEAD_DIM:(kvh * GQA + g + 1) * HEAD_DIM]
                                  for g in range(GQA)], axis=0)
            k = kv_refs[2 * seg][0, pl.ds(row0, blk), kv_lanes[kvh]].astype(BF)
            s = _dot_nt(q4, k)
            s_ref[kvh, b] = s
            m_ref[kvh] = jnp.maximum(m_ref[kvh], _fold_lanes(s, jnp.maximum))

    over_blocks(scores)
    for kvh in range(KV_HEADS):
        m_ref[kvh] = jnp.broadcast_to(jnp.max(m_ref[kvh], axis=-1, keepdims=True), m_ref.shape[1:])

    acc_ref[...] = jnp.zeros(acc_ref.shape, F32)
    ones = jnp.ones((blk, LANES), BF)

    def values(seg, row0, b):
        for kvh in range(KV_HEADS):
            v = kv_refs[2 * seg + 1][0, pl.ds(row0, blk), kv_lanes[kvh]].astype(BF)
            row_max = m_ref[kvh]
            e = jnp.concatenate(
                [jnp.exp2(s_ref[kvh, b, :, t * LANES:(t + 1) * LANES] - row_max).astype(BF)
                 for t in range(blk // LANES)], axis=1)
            acc_ref[kvh] += _dot(e, jnp.concatenate([v, ones], axis=1))

    over_blocks(values)
    pieces = []
    for kvh in range(KV_HEADS):
        att = acc_ref[kvh, :, :HEAD_DIM] / acc_ref[kvh, :, HEAD_DIM:]
        pieces += [att[g * tq:(g + 1) * tq].astype(BF) for g in range(GQA)]

    o = _dot(jnp.concatenate(pieces, axis=1), w_ref[...])
    o_ref[0] = x_ref[0] + gate_ref[0] * (_rms(o) * gain_ref[...])


def _attention(x3, q3, kv_segments, w_o, gain, mod):
    b, t, _ = x3.shape
    tq = TOKEN_TILE
    seg_lens = tuple(k.shape[1] for k, _ in kv_segments)
    blk = min(KV_BLOCK, *seg_lens)
    assert all(n % blk == 0 for n in seg_lens)
    n_blocks = sum(seg_lens) // blk
    rows = GQA * tq
    kv_w = KV_HEADS * HEAD_DIM
    in_specs = [pl.BlockSpec((1, tq, D_MODEL), lambda i, j: (i, j, 0)),
                pl.BlockSpec((1, tq, D_MODEL), lambda i, j: (i, j, 0))]
    args = [x3, q3]
    for k, v in kv_segments:
        in_specs += [pl.BlockSpec((1, k.shape[1], kv_w), lambda i, j: (i, 0, 0))] * 2
        args += [k, v]
    in_specs += [_resident(w_o.shape), _vec_spec(D_MODEL), mod.spec_batch(2)]
    args += [w_o, gain, mod.arr]
    return pl.pallas_call(
        functools.partial(_attn_kernel, seg_lens=seg_lens, blk=blk),
        out_shape=jax.ShapeDtypeStruct(x3.shape, F32),
        grid=(b, t // tq),
        in_specs=in_specs,
        out_specs=pl.BlockSpec((1, tq, D_MODEL), lambda i, j: (i, j, 0)),
        scratch_shapes=[pltpu.VMEM((KV_HEADS, n_blocks, rows, blk), F32),
                        pltpu.VMEM((KV_HEADS, rows, LANES), F32),
                        pltpu.VMEM((KV_HEADS, rows, HEAD_DIM + LANES), F32)],
        compiler_params=_params("parallel", "parallel"),
        name="attention",
    )(*args)


def _even_layer(x3, mod, norm_gain, w_in, tables, gn_gain, pool_w, pool_scale, w_out, state):
    proj, sstart, *s_fwd = _in_proj(x3, norm_gain[0:1], mod, w_in, tables, state)
    x3, s_fin = _mixer(x3, proj, sstart, mod, tables, gn_gain, pool_w, pool_scale, w_out, norm_gain[1:2], state,
                       s_fwd[0] if s_fwd else None)
    return x3.reshape(-1, D_MODEL), s_fin


def _odd_layer(x, b, mod, norm_gain, w_qkv, q_gain, k_gain, w_o, cache, rope_tables):
    t = x.shape[0] // b
    emit_cache = cache is None
    q, kb, vb, *kv_f32 = _qkv(x, norm_gain[0:1], mod, w_qkv, q_gain, k_gain, rope_tables, emit_cache)
    as3 = lambda a: a.reshape(b, t, -1)
    segments = [(as3(kb), as3(vb))]
    if cache is not None:
        segments.insert(0, tuple(a.reshape(b, -1, KV_HEADS * HEAD_DIM) for a in cache))
    x3 = _attention(as3(x), as3(q), segments, w_o, norm_gain[1:2], mod)
    return x3.reshape(-1, D_MODEL), kv_f32


def kernel(x_prompt, x_sample, state_ret, cache_k, cache_v, c, c_ctx, w_ada, b_ada, norm_gain, w_ffn_in, w_ffn_out,
           w_in_even, ret_decay_logit, ret_gn_gain, pool_w, pool_scale, w_out_even, w_qkv, q_norm_gain, k_norm_gain,
           w_o):
    bp, tp, _ = x_prompt.shape
    bs, ts, _ = x_sample.shape
    depth = w_ada.shape[0]
    assert bs + 1 <= MOD_ROWS and depth == 2

    cvec = jnp.zeros((MOD_ROWS, D_MODEL), F32).at[0].set(c_ctx).at[1:1 + bs].set(c)
    mod_flat = _modulation(cvec, w_ada, b_ada).reshape(depth * MOD_ROWS * MOD_PARTS, 1, D_MODEL)
    bf = lambda w: w.astype(BF)

    mod_p = _Mod(mod_flat, 0, None)
    mod_s = _Mod(mod_flat, 0, ts)
    tables = _decay_tables(jax.nn.log_sigmoid(ret_decay_logit[0].astype(F32)))
    even = functools.partial(_even_layer, norm_gain=norm_gain[0], w_in=bf(w_in_even[0]), tables=tables,
                             gn_gain=ret_gn_gain[0:1], pool_w=bf(pool_w[0]), pool_scale=pool_scale[0:1],
                             w_out=bf(w_out_even[0]))
    xp, s_fin = even(x_prompt, mod_p, state=None)
    xs, _ = even(x_sample, mod_s, state=state_ret)
    ffn0 = functools.partial(_ffn, gain_in=norm_gain[0, 2:3], gain_out=norm_gain[0, 3:4], w_in=bf(w_ffn_in[0]),
                             w_out=bf(w_ffn_out[0]))
    xp = ffn0(xp, mod=mod_p)
    xs = ffn0(xs, mod=mod_s)

    mod_p = _Mod(mod_flat, 1, None)
    mod_s = _Mod(mod_flat, 1, ts)
    odd = functools.partial(_odd_layer, norm_gain=norm_gain[1], w_qkv=bf(w_qkv[0]), q_gain=q_norm_gain[0:1],
                            k_gain=k_norm_gain[0:1], w_o=bf(w_o[0]))
    xp, (kp, vp) = odd(xp, bp, mod_p, cache=None, rope_tables=None)
    xs, _ = odd(xs, bs, mod_s, cache=(cache_k[:, 0], cache_v[:, 0]), rope_tables=_rope_tables(ts))
    ffn1 = functools.partial(_ffn, gain_in=norm_gain[1, 2:3], gain_out=norm_gain[1, 3:4], w_in=bf(w_ffn_in[1]),
                             w_out=bf(w_ffn_out[1]))
    xp = ffn1(xp, mod=mod_p)
    xs = ffn1(xs, mod=mod_s)

    new_state_ret = s_fin.reshape(bp, 1, 2, H_RET, DK_RET, DV_RET)
    new_cache_k = kp.reshape(bp, 1, tp, KV_HEADS, HEAD_DIM)
    new_cache_v = vp.reshape(bp, 1, tp, KV_HEADS, HEAD_DIM)
    return (xp.reshape(bp, tp, D_MODEL), xs.reshape(bs, ts, D_MODEL), new_state_ret, new_cache_k, new_cache_v)
```

```python
import functools

import jax
import jax.numpy as jnp
from jax import lax
from jax.experimental import pallas as pl
from jax.experimental.pallas import tpu as pltpu

D_MODEL = 1024
EPS = 1e-6
GRID_W = 64
H_RET = 4
DK_RET = 128
DV_RET = 256
POOL_WINDOWS = (2, 4, 8, 16)
POOL_GROUP_DIM = 128
POOL_WIDTH = 512
POOL_PAD = 8
HALO_BLOCK = 16
Q_OFF, K_OFF, V_OFF, G_OFF, U_OFF = 0, 512, 1024, 2048, 3072
EVEN_IN = 3584
HEAD_DIM = 128
N_HEADS = 8
KV_HEADS = 2
GQA = N_HEADS // KV_HEADS
ROPE_BASE = 10000.0
ROPE_PAIRS = 32
QKV_OUT = (N_HEADS + 2 * KV_HEADS) * HEAD_DIM
D_FF = 2816
FF_CHUNKS = ((0, 1536), (1536, 1280))
MOD_ROWS = 8
MOD_PARTS = 6

BF = jnp.bfloat16
F32 = jnp.float32
VMEM_LIMIT_BYTES = 52 * 1024 * 1024
TOKEN_TILE = 256
FFN_TILE = 512


def _params(*sem):
    return pltpu.CompilerParams(dimension_semantics=sem, vmem_limit_bytes=VMEM_LIMIT_BYTES)


def _resident(shape):
    nd = len(shape)
    return pl.BlockSpec(shape, lambda *_: (0,) * nd, pipeline_mode=pl.Buffered(1))


def _rms(x):
    return x * lax.rsqrt(jnp.mean(x * x, axis=-1, keepdims=True) + EPS)


def _modnorm(x, gain, sc, sh):
    return (_rms(x) * gain) * (1.0 + sc) + sh


def _silu(x):
    return x * jax.nn.sigmoid(x)


def _dot(a, b):
    return jnp.dot(a, b, preferred_element_type=F32)


def _dot_nt(a, b):
    return lax.dot_general(a, b, (((1,), (1,)), ((), ())), preferred_element_type=F32)


def _dot_tn(a, b):
    return lax.dot_general(a, b, (((0,), (0,)), ((), ())), preferred_element_type=F32)


def _mod_kernel(c_ref, w_ref, b_ref, o_ref):
    s = _silu(c_ref[...]).astype(BF)
    o_ref[0] = _dot(s, w_ref[0].astype(BF)) + b_ref[0]


def _modulation(cvec, w_ada, b_ada):
    depth = w_ada.shape[0]
    n = w_ada.shape[2]
    tn = 1536
    return pl.pallas_call(
        _mod_kernel,
        out_shape=jax.ShapeDtypeStruct((depth, MOD_ROWS, n), F32),
        grid=(depth, n // tn),
        in_specs=[
            pl.BlockSpec((MOD_ROWS, D_MODEL), lambda l, j: (0, 0)),
            pl.BlockSpec((1, D_MODEL, tn), lambda l, j: (l, 0, j)),
            pl.BlockSpec((1, 1, tn), lambda l, j: (l, 0, j)),
        ],
        out_specs=pl.BlockSpec((1, MOD_ROWS, tn), lambda l, j: (l, 0, j)),
        compiler_params=_params("parallel", "parallel"),
        name="adaln_mod",
    )(cvec, w_ada, b_ada.reshape(depth, 1, n))


class _Mod:
    def __init__(self, mod_flat, layer, tokens_per_row):
        self.arr = mod_flat
        self.layer = layer
        self.tokens_per_row = tokens_per_row

    def _block(self, row, part):
        return ((self.layer * MOD_ROWS + row) * MOD_PARTS + part, 0, 0)

    def spec(self, part, tile):
        def index(i, *_):
            return self._block(0 if self.tokens_per_row is None else 1 + (i * tile) // self.tokens_per_row, part)

        return pl.BlockSpec((1, 1, D_MODEL), index)

    def spec_batch(self, part):
        def index(b, *_):
            return self._block(0 if self.tokens_per_row is None else 1 + b, part)

        return pl.BlockSpec((1, 1, D_MODEL), index)


def _row_spec(tile, width):
    return pl.BlockSpec((tile, width), lambda i: (i, 0))


def _vec_spec(width):
    return pl.BlockSpec((1, width), lambda *_: (0, 0))


def _positions(n):
    return lax.broadcasted_iota(jnp.int32, (n, 1), 0).astype(F32)


XI_F, XI_B, ZETA_F, ZETA_B = range(4)


def _decay_kernel(lg_ref, dmask_ref, vec_ref, dec_ref):
    hd = pl.program_id(0)
    lgf = lg_ref[0, hd]
    lgb = lg_ref[1, hd]
    c_len = TOKEN_TILE
    qscale = DK_RET ** -0.5
    diff = _positions(c_len) - lax.broadcasted_iota(jnp.int32, (1, c_len), 1).astype(F32)
    dmask_ref[0] = qscale * (jnp.where(diff >= 0, jnp.exp(jnp.maximum(diff, 0.0) * lgf), 0.0)
                             + jnp.where(diff <= 0, jnp.exp(jnp.maximum(-diff, 0.0) * lgb), 0.0))
    n = lax.broadcasted_iota(jnp.int32, (c_len, DK_RET), 0).astype(F32)
    vec_ref[0, XI_F] = qscale * jnp.exp((n + 1.0) * lgf)
    vec_ref[0, XI_B] = qscale * jnp.exp((c_len - n) * lgb)
    vec_ref[0, ZETA_F] = jnp.exp((c_len - 1.0 - n) * lgf)
    vec_ref[0, ZETA_B] = jnp.exp(n * lgb)
    chunk = jnp.full((1, DV_RET), c_len, F32)
    dec_ref[0, 0] = jnp.exp(chunk * lgf)
    dec_ref[0, 1] = jnp.exp(chunk * lgb)


def _decay_tables(lg):
    c_len = TOKEN_TILE
    return pl.pallas_call(
        _decay_kernel,
        out_shape=[jax.ShapeDtypeStruct((H_RET, c_len, c_len), F32),
                   jax.ShapeDtypeStruct((H_RET, 4, c_len, DK_RET), F32),
                   jax.ShapeDtypeStruct((H_RET, 2, 1, DV_RET), F32)],
        grid_spec=pltpu.PrefetchScalarGridSpec(
            num_scalar_prefetch=1, grid=(H_RET,), in_specs=[],
            out_specs=[pl.BlockSpec((1, c_len, c_len), lambda h, lg: (h, 0, 0)),
                       pl.BlockSpec((1, 4, c_len, DK_RET), lambda h, lg: (h, 0, 0, 0)),
                       pl.BlockSpec((1, 2, 1, DV_RET), lambda h, lg: (h, 0, 0, 0))]),
        compiler_params=_params("arbitrary"),
        name="decay_tables",
    )(lg)


def _in_proj_kernel(x_ref, gain_ref, sc_ref, sh_ref, w_ref, vec_ref, dec_ref, *rest, nt, has_s0, emit_state):
    rest = list(rest)
    s0_ref = rest.pop(0) if has_s0 else None
    proj_ref = rest.pop(0)
    sstart_ref = rest.pop(0)
    sfin_ref = rest.pop(0) if emit_state else None
    (s_ref,) = rest
    j = pl.program_id(1)

    @pl.when(j == 0)
    def _():
        s_ref[...] = s0_ref[0, 0, 0] if has_s0 else jnp.zeros_like(s_ref)

    h = _modnorm(x_ref[0], gain_ref[...], sc_ref[0], sh_ref[0]).astype(BF)
    proj = _dot(h, w_ref[...])
    proj_ref[0] = proj.astype(BF)

    for hd in range(H_RET):
        sstart_ref[0, 0, hd] = s_ref[hd].astype(BF)
        k = proj[:, K_OFF + hd * DK_RET:K_OFF + (hd + 1) * DK_RET]
        v = proj[:, V_OFF + hd * DV_RET:V_OFF + (hd + 1) * DV_RET]
        s_ref[hd] = s_ref[hd] * dec_ref[hd, 0] + _dot_tn((k * vec_ref[hd, ZETA_F]).astype(BF), v.astype(BF))

    if emit_state:
        @pl.when(j == nt - 1)
        def _():
            sfin_ref[0] = s_ref[...]


def _in_proj(x3, gain, mod, w, tables, state):
    b, t, _ = x3.shape
    tm = TOKEN_TILE
    nt = t // tm
    has_s0 = state is not None
    emit_state = not has_s0
    _, vec, dec = tables
    in_specs = [pl.BlockSpec((1, tm, D_MODEL), lambda i, j: (i, j, 0)), _vec_spec(D_MODEL),
                mod.spec_batch(1), mod.spec_batch(0), _resident(w.shape), _resident(vec.shape), _resident(dec.shape)]
    args = [x3, gain, mod.arr, mod.arr, w, vec, dec]
    if has_s0:
        in_specs.append(pl.BlockSpec((1, 1, 1, H_RET, DK_RET, DV_RET), lambda i, j: (i, 0, 0, 0, 0, 0)))
        args.append(state)
    out_shape = [jax.ShapeDtypeStruct((b, t, EVEN_IN), BF),
                 jax.ShapeDtypeStruct((b, nt, H_RET, DK_RET, DV_RET), BF)]
    out_specs = [pl.BlockSpec((1, tm, EVEN_IN), lambda i, j: (i, j, 0)),
                 pl.BlockSpec((1, 1, H_RET, DK_RET, DV_RET), lambda i, j: (i, j, 0, 0, 0))]
    if emit_state:
        out_shape.append(jax.ShapeDtypeStruct((b, H_RET, DK_RET, DV_RET), F32))
        out_specs.append(pl.BlockSpec((1, H_RET, DK_RET, DV_RET), lambda i, j: (i, 0, 0, 0)))
    return pl.pallas_call(
        functools.partial(_in_proj_kernel, nt=nt, has_s0=has_s0, emit_state=emit_state),
        out_shape=out_shape,
        grid=(b, nt),
        in_specs=in_specs,
        out_specs=out_specs,
        scratch_shapes=[pltpu.VMEM((H_RET, DK_RET, DV_RET), F32)],
        compiler_params=_params("parallel", "arbitrary"),
        name="in_proj",
    )(*args)


def _mixer_kernel(x_ref, proj_ref, prev_ref, next_ref, sstart_ref, dmask_ref, vec_ref, dec_ref, gn_ref, pw_ref,
                  ps_ref, wout_ref, gain_ref, gate_ref, *rest, nt, t_len, has_s0, emit_state):
    rest = list(rest)
    s0_ref = rest.pop(0) if has_s0 else None
    sfwd_ref = rest.pop(0) if emit_state else None
    o_ref = rest.pop(0)
    sfin_ref = rest.pop(0) if emit_state else None
    s_ref, pad_ref = rest
    j = pl.program_id(1)
    tile = nt - 1 - j
    c_len = TOKEN_TILE

    @pl.when(j == 0)
    def _():
        s_ref[...] = s0_ref[0, 0, 0] if has_s0 else jnp.zeros_like(s_ref)

    gn = gn_ref[...]
    pieces = []
    for hd in range(H_RET):
        q = proj_ref[0, :, Q_OFF + hd * DK_RET:Q_OFF + (hd + 1) * DK_RET]
        k = proj_ref[0, :, K_OFF + hd * DK_RET:K_OFF + (hd + 1) * DK_RET]
        v = proj_ref[0, :, V_OFF + hd * DV_RET:V_OFF + (hd + 1) * DV_RET]
        g = proj_ref[0, :, G_OFF + hd * DV_RET:G_OFF + (hd + 1) * DV_RET].astype(F32)
        scores = _dot_nt(q, k) * dmask_ref[hd]
        inner = _dot(scores.astype(BF), v)
        q32 = q.astype(F32)
        q_both = jnp.concatenate([(q32 * vec_ref[hd, XI_F]).astype(BF), (q32 * vec_ref[hd, XI_B]).astype(BF)],
                                 axis=1)
        s_both = jnp.concatenate([sstart_ref[0, 0, hd], s_ref[hd].astype(BF)], axis=0)
        y = inner + _dot(q_both, s_both)
        s_ref[hd] = s_ref[hd] * dec_ref[hd, 1] + _dot_tn((k.astype(F32) * vec_ref[hd, ZETA_B]).astype(BF), v)

        mu = jnp.mean(y, axis=-1, keepdims=True)
        yc = y - mu
        var = jnp.mean(yc * yc, axis=-1, keepdims=True)
        yn = (yc * lax.rsqrt(var + EPS)) * gn[:, hd * DV_RET:(hd + 1) * DV_RET]
        pieces.append((_silu(g) * yn).astype(BF))

    if emit_state:
        @pl.when(j == nt - 1)
        def _():
            sfin_ref[0, 0] = sfwd_ref[0]
            sfin_ref[0, 1] = s_ref[...]

    prev_rows = prev_ref[0].astype(F32)[HALO_BLOCK - POOL_PAD:]
    next_rows = next_ref[0].astype(F32)[:POOL_PAD]
    pad_ref[0:POOL_PAD, :] = jnp.where(tile > 0, prev_rows, 0.0)
    pad_ref[POOL_PAD:POOL_PAD + c_len, :] = proj_ref[0, :, U_OFF:U_OFF + POOL_WIDTH].astype(F32)
    pad_ref[POOL_PAD + c_len:2 * POOL_PAD + c_len, :] = jnp.where(tile < nt - 1, next_rows, 0.0)
    pos = tile * c_len + lax.broadcasted_iota(jnp.int32, (c_len, 1), 0)
    for grp, window in enumerate(POOL_WINDOWS):
        half = window // 2
        lanes = slice(grp * POOL_GROUP_DIM, (grp + 1) * POOL_GROUP_DIM)
        acc = pad_ref[POOL_PAD - half:POOL_PAD - half + c_len, lanes]
        for off in range(-half + 1, half):
            acc = acc + pad_ref[POOL_PAD + off:POOL_PAD + off + c_len, lanes]
        cnt = (jnp.minimum(pos + half, t_len) - jnp.maximum(pos - half, 0)).astype(F32)
        d = acc / cnt - pad_ref[POOL_PAD:POOL_PAD + c_len, lanes]
        pieces.append((_dot(d.astype(BF), pw_ref[grp]) * ps_ref[:, lanes]).astype(BF))

    o = _dot(jnp.concatenate(pieces, axis=1), wout_ref[...])
    o_ref[0] = x_ref[0] + gate_ref[0] * (_rms(o) * gain_ref[...])


def _mixer(x3, proj, sstart, mod, tables, gn_gain, pool_w, pool_scale, w_out, gain, state, s_fwd):
    b, t, _ = x3.shape
    tm = TOKEN_TILE
    nt = t // tm
    halo_per_tile = tm // HALO_BLOCK
    n_halo = t // HALO_BLOCK
    has_s0 = state is not None
    emit_state = not has_s0
    rev = lambda j: nt - 1 - j
    in_specs = [
        pl.BlockSpec((1, tm, D_MODEL), lambda i, j: (i, rev(j), 0)),
        pl.BlockSpec((1, tm, EVEN_IN), lambda i, j: (i, rev(j), 0)),
        pl.BlockSpec((1, HALO_BLOCK, POOL_WIDTH),
                     lambda i, j: (i, jnp.maximum(rev(j) * halo_per_tile - 1, 0), U_OFF // POOL_WIDTH)),
        pl.BlockSpec((1, HALO_BLOCK, POOL_WIDTH),
                     lambda i, j: (i, jnp.minimum((rev(j) + 1) * halo_per_tile, n_halo - 1), U_OFF // POOL_WIDTH)),
        pl.BlockSpec((1, 1, H_RET, DK_RET, DV_RET), lambda i, j: (i, rev(j), 0, 0, 0)),
        *[_resident(a.shape) for a in tables],
        _vec_spec(H_RET * DV_RET), _resident(pool_w.shape), _vec_spec(POOL_WIDTH), _resident(w_out.shape),
        _vec_spec(D_MODEL), mod.spec_batch(2),
    ]
    args = [x3, proj, proj, proj, sstart, *tables, gn_gain, pool_w, pool_scale, w_out, gain, mod.arr]
    out_shape = [jax.ShapeDtypeStruct((b, t, D_MODEL), F32)]
    out_specs = [pl.BlockSpec((1, tm, D_MODEL), lambda i, j: (i, rev(j), 0))]
    if has_s0:
        in_specs.append(pl.BlockSpec((1, 1, 1, H_RET, DK_RET, DV_RET), lambda i, j: (i, 0, 1, 0, 0, 0)))
        args.append(state)
    else:
        in_specs.append(pl.BlockSpec((1, H_RET, DK_RET, DV_RET), lambda i, j: (i, 0, 0, 0)))
        args.append(s_fwd)
        out_shape.append(jax.ShapeDtypeStruct((b, 2, H_RET, DK_RET, DV_RET), F32))
        out_specs.append(pl.BlockSpec((1, 2, H_RET, DK_RET, DV_RET), lambda i, j: (i, 0, 0, 0, 0)))
    outs = pl.pallas_call(
        functools.partial(_mixer_kernel, nt=nt, t_len=t, has_s0=has_s0, emit_state=emit_state),
        out_shape=out_shape,
        grid=(b, nt),
        in_specs=in_specs,
        out_specs=out_specs,
        scratch_shapes=[pltpu.VMEM((H_RET, DK_RET, DV_RET), F32),
                        pltpu.VMEM((tm + 2 * POOL_PAD, POOL_WIDTH), F32)],
        compiler_params=_params("parallel", "arbitrary"),
        name="mixer",
    )(*args)
    return outs if emit_state else (outs[0], None)


def _swiglu(h, win_ref, wout_ref):
    f = None
    for start, width in FF_CHUNKS:
        a = _dot(h, win_ref[:, start:start + width])
        b = _dot(h, win_ref[:, D_FF + start:D_FF + start + width])
        part = _dot((_silu(a) * b).astype(BF), wout_ref[start:start + width, :])
        f = part if f is None else f + part
    return f


def _ffn_kernel(xp_ref, xs_ref, gain_in_ref, scp_ref, shp_ref, gatep_ref, scs_ref, shs_ref, gates_ref, gain_out_ref,
                win_ref, wout_ref, op_ref, os_ref):
    hp = _modnorm(xp_ref[...], gain_in_ref[...], scp_ref[0], shp_ref[0]).astype(BF)
    hs = _modnorm(xs_ref[...], gain_in_ref[...], scs_ref[0], shs_ref[0]).astype(BF)
    fp = _swiglu(hp, win_ref, wout_ref)
    op_ref[...] = xp_ref[...] + gatep_ref[0] * (_rms(fp) * gain_out_ref[...])
    fs = _swiglu(hs, win_ref, wout_ref)
    os_ref[...] = xs_ref[...] + gates_ref[0] * (_rms(fs) * gain_out_ref[...])


def _ffn(xp, xs, gain_in, gain_out, mod_p, mod_s, w_in, w_out):
    n_tok = xp.shape[0]
    assert xs.shape == xp.shape
    tm = FFN_TILE
    mods = [m.spec(part, tm) for m in (mod_p, mod_s) for part in (4, 3, 5)]
    return pl.pallas_call(
        _ffn_kernel,
        out_shape=[jax.ShapeDtypeStruct((n_tok, D_MODEL), F32)] * 2,
        grid=(n_tok // tm,),
        in_specs=[_row_spec(tm, D_MODEL), _row_spec(tm, D_MODEL), _vec_spec(D_MODEL), *mods, _vec_spec(D_MODEL),
                  _resident(w_in.shape), _resident(w_out.shape)],
        out_specs=[_row_spec(tm, D_MODEL)] * 2,
        compiler_params=_params("parallel"),
        name="ffn",
    )(xp, xs, gain_in, *([mod_p.arr] * 6), gain_out, w_in, w_out)


def _rope(x, cs, sn):
    lane = lax.broadcasted_iota(jnp.int32, x.shape, 1)
    first = (lane % (2 * ROPE_PAIRS)) < ROPE_PAIRS
    partner = jnp.where(first, pltpu.roll(x, HEAD_DIM - ROPE_PAIRS, 1), pltpu.roll(x, ROPE_PAIRS, 1))
    return x * cs + partner * sn


Q_PRESCALE = HEAD_DIM ** -0.5 * 1.4426950408889634


def _qkv_kernel(x_ref, gain_ref, sc_ref, sh_ref, w_ref, qg_ref, kg_ref, *rest, rope, emit_cache):
    rest = list(rest)
    cs_ref = rest.pop(0) if rope else None
    sn_ref = rest.pop(0) if rope else None
    q_ref, kb_ref, vb_ref = rest[:3]
    kf_ref, vf_ref = rest[3:] if emit_cache else (None, None)
    h = _modnorm(x_ref[...], gain_ref[...], sc_ref[0], sh_ref[0]).astype(BF)
    qkv = _dot(h, w_ref[...])

    def head(idx, g_ref):
        xh = _rms(qkv[:, idx * HEAD_DIM:(idx + 1) * HEAD_DIM]) * g_ref[...]
        return _rope(xh, cs_ref[...], sn_ref[...]) if rope else xh

    for i in range(N_HEADS):
        q_ref[:, i * HEAD_DIM:(i + 1) * HEAD_DIM] = (head(i, qg_ref) * Q_PRESCALE).astype(BF)
    for i in range(KV_HEADS):
        lanes = slice(i * HEAD_DIM, (i + 1) * HEAD_DIM)
        kh = head(N_HEADS + i, kg_ref)
        kb_ref[:, lanes] = kh.astype(BF)
        if emit_cache:
            kf_ref[:, lanes] = kh
    v = qkv[:, (N_HEADS + KV_HEADS) * HEAD_DIM:]
    vb_ref[...] = v.astype(BF)
    if emit_cache:
        vf_ref[...] = v


def _qkv(x, gain, mod, w, q_gain, k_gain, rope_tables, emit_cache):
    n_tok = x.shape[0]
    tm = TOKEN_TILE
    rope = rope_tables is not None
    in_specs = [_row_spec(tm, D_MODEL), _vec_spec(D_MODEL), mod.spec(1, tm), mod.spec(0, tm), _resident(w.shape),
                _vec_spec(HEAD_DIM), _vec_spec(HEAD_DIM)]
    args = [x, gain, mod.arr, mod.arr, w, q_gain, k_gain]
    if rope:
        tiles_per_seq = rope_tables[0].shape[0] // tm
        in_specs += [pl.BlockSpec((tm, HEAD_DIM), lambda i: (i % tiles_per_seq, 0))] * 2
        args += list(rope_tables)
    kv_w = KV_HEADS * HEAD_DIM
    widths = [(D_MODEL, BF), (kv_w, BF), (kv_w, BF)] + ([(kv_w, F32), (kv_w, F32)] if emit_cache else [])
    return pl.pallas_call(
        functools.partial(_qkv_kernel, rope=rope, emit_cache=emit_cache),
        out_shape=[jax.ShapeDtypeStruct((n_tok, wd), dt) for wd, dt in widths],
        grid=(n_tok // tm,),
        in_specs=in_specs,
        out_specs=[_row_spec(tm, wd) for wd, _ in widths],
        compiler_params=_params("parallel"),
        name="qkv_proj",
    )(*args)


def _rope_tables(t):
    pos = jnp.arange(t, dtype=jnp.int32)
    row = (pos // GRID_W).astype(F32)
    col = (pos % GRID_W).astype(F32)
    freqs = ROPE_BASE ** (-jnp.arange(ROPE_PAIRS, dtype=F32) / ROPE_PAIRS)
    ang_r = row[:, None] * freqs[None, :]
    ang_c = col[:, None] * freqs[None, :]
    cs = jnp.concatenate([jnp.cos(ang_r)] * 2 + [jnp.cos(ang_c)] * 2, axis=-1)
    sn = jnp.concatenate([-jnp.sin(ang_r), jnp.sin(ang_r), -jnp.sin(ang_c), jnp.sin(ang_c)], axis=-1)
    return cs, sn


KV_BLOCK = 512
LANES = 128


def _fold_lanes(x, op):
    out = x[:, :LANES]
    for i in range(1, x.shape[1] // LANES):
        out = op(out, x[:, i * LANES:(i + 1) * LANES])
    return out


def _attn_kernel(x_ref, q_ref, *rest, seg_lens, blk):
    n_seg = len(seg_lens)
    kv_refs = rest[:2 * n_seg]
    w_ref, gain_ref, gate_ref, o_ref, s_ref, m_ref, acc_ref = rest[2 * n_seg:]
    tq = q_ref.shape[1]
    kv_lanes = [slice(kvh * HEAD_DIM, (kvh + 1) * HEAD_DIM) for kvh in range(KV_HEADS)]

    def over_blocks(body):
        base = 0
        for seg, n in enumerate(seg_lens):
            nb = n // blk
            if nb == 1:
                body(seg, 0, base)
            else:
                def step(i, carry, seg=seg, base=base):
                    body(seg, pl.multiple_of(i * blk, blk), base + i)
                    return carry

                lax.fori_loop(0, nb, step, 0, unroll=True)
            base += nb

    m_ref[...] = jnp.full(m_ref.shape, -jnp.inf, F32)

    def scores(seg, row0, b):
        for kvh in range(KV_HEADS):
            q4 = jnp.concatenate([q_ref[0, :, (kvh * GQA + g) * HEAD_DIM:(kvh * GQA + g + 1) * HEAD_DIM]
                                  for g in range(GQA)], axis=0)
            k = kv_refs[2 * seg][0, pl.ds(row0, blk), kv_lanes[kvh]].astype(BF)
            s = _dot_nt(q4, k)
            s_ref[kvh, b] = s
            m_ref[kvh] = jnp.maximum(m_ref[kvh], _fold_lanes(s, jnp.maximum))

    over_blocks(scores)
    for kvh in range(KV_HEADS):
        m_ref[kvh] = jnp.broadcast_to(jnp.max(m_ref[kvh], axis=-1, keepdims=True), m_ref.shape[1:])

    acc_ref[...] = jnp.zeros(acc_ref.shape, F32)
    ones = jnp.ones((blk, LANES), BF)

    def values(seg, row0, b):
        for kvh in range(KV_HEADS):
            v = kv_refs[2 * seg + 1][0, pl.ds(row0, blk), kv_lanes[kvh]].astype(BF)
            row_max = m_ref[kvh]
            e = jnp.concatenate(
                [jnp.exp2(s_ref[kvh, b, :, t * LANES:(t + 1) * LANES] - row_max).astype(BF)
                 for t in range(blk // LANES)], axis=1)
            acc_ref[kvh] += _dot(e, jnp.concatenate([v, ones], axis=1))

    over_blocks(values)
    pieces = []
    for kvh in range(KV_HEADS):
        att = acc_ref[kvh, :, :HEAD_DIM] / acc_ref[kvh, :, HEAD_DIM:]
        pieces += [att[g * tq:(g + 1) * tq].astype(BF) for g in range(GQA)]

    o = _dot(jnp.concatenate(pieces, axis=1), w_ref[...])
    o_ref[0] = x_ref[0] + gate_ref[0] * (_rms(o) * gain_ref[...])


def _attention(x3, q3, kv_segments, w_o, gain, mod):
    b, t, _ = x3.shape
    tq = TOKEN_TILE
    seg_lens = tuple(k.shape[1] for k, _ in kv_segments)
    blk = min(KV_BLOCK, *seg_lens)
    assert all(n % blk == 0 for n in seg_lens)
    n_blocks = sum(seg_lens) // blk
    rows = GQA * tq
    kv_w = KV_HEADS * HEAD_DIM
    in_specs = [pl.BlockSpec((1, tq, D_MODEL), lambda i, j: (i, j, 0)),
                pl.BlockSpec((1, tq, D_MODEL), lambda i, j: (i, j, 0))]
    args = [x3, q3]
    for k, v in kv_segments:
        in_specs += [pl.BlockSpec((1, k.shape[1], kv_w), lambda i, j: (i, 0, 0))] * 2
        args += [k, v]
    in_specs += [_resident(w_o.shape), _vec_spec(D_MODEL), mod.spec_batch(2)]
    args += [w_o, gain, mod.arr]
    return pl.pallas_call(
        functools.partial(_attn_kernel, seg_lens=seg_lens, blk=blk),
        out_shape=jax.ShapeDtypeStruct(x3.shape, F32),
        grid=(b, t // tq),
        in_specs=in_specs,
        out_specs=pl.BlockSpec((1, tq, D_MODEL), lambda i, j: (i, j, 0)),
        scratch_shapes=[pltpu.VMEM((KV_HEADS, n_blocks, rows, blk), F32),
                        pltpu.VMEM((KV_HEADS, rows, LANES), F32),
                        pltpu.VMEM((KV_HEADS, rows, HEAD_DIM + LANES), F32)],
        compiler_params=_params("parallel", "parallel"),
        name="attention",
    )(*args)


def _even_layer(x3, mod, norm_gain, w_in, tables, gn_gain, pool_w, pool_scale, w_out, state):
    proj, sstart, *s_fwd = _in_proj(x3, norm_gain[0:1], mod, w_in, tables, state)
    x3, s_fin = _mixer(x3, proj, sstart, mod, tables, gn_gain, pool_w, pool_scale, w_out, norm_gain[1:2], state,
                       s_fwd[0] if s_fwd else None)
    return x3.reshape(-1, D_MODEL), s_fin


def _odd_layer(x, b, mod, norm_gain, w_qkv, q_gain, k_gain, w_o, cache, rope_tables):
    t = x.shape[0] // b
    emit_cache = cache is None
    q, kb, vb, *kv_f32 = _qkv(x, norm_gain[0:1], mod, w_qkv, q_gain, k_gain, rope_tables, emit_cache)
    as3 = lambda a: a.reshape(b, t, -1)
    segments = [(as3(kb), as3(vb))]
    if cache is not None:
        segments.insert(0, tuple(a.reshape(b, -1, KV_HEADS * HEAD_DIM) for a in cache))
    x3 = _attention(as3(x), as3(q), segments, w_o, norm_gain[1:2], mod)
    return x3.reshape(-1, D_MODEL), kv_f32


def kernel(x_prompt, x_sample, state_ret, cache_k, cache_v, c, c_ctx, w_ada, b_ada, norm_gain, w_ffn_in, w_ffn_out,
           w_in_even, ret_decay_logit, ret_gn_gain, pool_w, pool_scale, w_out_even, w_qkv, q_norm_gain, k_norm_gain,
           w_o):
    bp, tp, _ = x_prompt.shape
    bs, ts, _ = x_sample.shape
    depth = w_ada.shape[0]
    assert bs + 1 <= MOD_ROWS and depth == 2

    cvec = jnp.zeros((MOD_ROWS, D_MODEL), F32).at[0].set(c_ctx).at[1:1 + bs].set(c)
    mod_flat = _modulation(cvec, w_ada, b_ada).reshape(depth * MOD_ROWS * MOD_PARTS, 1, D_MODEL)
    bf = lambda w: w.astype(BF)

    mod_p = _Mod(mod_flat, 0, None)
    mod_s = _Mod(mod_flat, 0, ts)
    tables = _decay_tables(jax.nn.log_sigmoid(ret_decay_logit[0].astype(F32)))
    even = functools.partial(_even_layer, norm_gain=norm_gain[0], w_in=bf(w_in_even[0]), tables=tables,
                             gn_gain=ret_gn_gain[0:1], pool_w=bf(pool_w[0]), pool_scale=pool_scale[0:1],
                             w_out=bf(w_out_even[0]))
    xp, s_fin = even(x_prompt, mod_p, state=None)
    xs, _ = even(x_sample, mod_s, state=state_ret)
    xp, xs = _ffn(xp, xs, norm_gain[0, 2:3], norm_gain[0, 3:4], mod_p, mod_s, bf(w_ffn_in[0]), bf(w_ffn_out[0]))

    mod_p = _Mod(mod_flat, 1, None)
    mod_s = _Mod(mod_flat, 1, ts)
    odd = functools.partial(_odd_layer, norm_gain=norm_gain[1], w_qkv=bf(w_qkv[0]), q_gain=q_norm_gain[0:1],
                            k_gain=k_norm_gain[0:1], w_o=bf(w_o[0]))
    xp, (kp, vp) = odd(xp, bp, mod_p, cache=None, rope_tables=None)
    xs, _ = odd(xs, bs, mod_s, cache=(cache_k[:, 0], cache_v[:, 0]), rope_tables=_rope_tables(ts))
    xp, xs = _ffn(xp, xs, norm_gain[1, 2:3], norm_gain[1, 3:4], mod_p, mod_s, bf(w_ffn_in[1]), bf(w_ffn_out[1]))

    new_state_ret = s_fin.reshape(bp, 1, 2, H_RET, DK_RET, DV_RET)
    new_cache_k = kp.reshape(bp, 1, tp, KV_HEADS, HEAD_DIM)
    new_cache_v = vp.reshape(bp, 1, tp, KV_HEADS, HEAD_DIM)
    return (xp.reshape(bp, tp, D_MODEL), xs.reshape(bs, ts, D_MODEL), new_state_ret, new_cache_k, new_cache_v)
```

```python
import functools

import jax
import jax.numpy as jnp
from jax import lax
from jax.experimental import pallas as pl
from jax.experimental.pallas import tpu as pltpu

D_MODEL = 1024
EPS = 1e-6
GRID_W = 64
H_RET = 4
DK_RET = 128
DV_RET = 256
POOL_WINDOWS = (2, 4, 8, 16)
POOL_GROUP_DIM = 128
POOL_WIDTH = 512
POOL_PAD = 8
HALO_BLOCK = 16
Q_OFF, K_OFF, V_OFF, G_OFF, U_OFF = 0, 512, 1024, 2048, 3072
EVEN_IN = 3584
HEAD_DIM = 128
N_HEADS = 8
KV_HEADS = 2
GQA = N_HEADS // KV_HEADS
ROPE_BASE = 10000.0
ROPE_PAIRS = 32
QKV_OUT = (N_HEADS + 2 * KV_HEADS) * HEAD_DIM
D_FF = 2816
FF_W = 256
N_FF = D_FF // FF_W
MOD_ROWS = 8
MOD_PARTS = 6

BF = jnp.bfloat16
F32 = jnp.float32
VMEM_LIMIT_BYTES = 52 * 1024 * 1024
TOKEN_TILE = 256
FFN_TILE = 512


def _params(*sem):
    return pltpu.CompilerParams(dimension_semantics=sem, vmem_limit_bytes=VMEM_LIMIT_BYTES)


def _resident(shape):
    nd = len(shape)
    return pl.BlockSpec(shape, lambda *_: (0,) * nd, pipeline_mode=pl.Buffered(1))


def _rms(x):
    return x * lax.rsqrt(jnp.mean(x * x, axis=-1, keepdims=True) + EPS)


def _modnorm(x, gain, sc, sh):
    return (_rms(x) * gain) * (1.0 + sc) + sh


def _silu(x):
    return x * jax.nn.sigmoid(x)


def _dot(a, b):
    return jnp.dot(a, b, preferred_element_type=F32)


def _dot_nt(a, b):
    return lax.dot_general(a, b, (((1,), (1,)), ((), ())), preferred_element_type=F32)


def _dot_tn(a, b):
    return lax.dot_general(a, b, (((0,), (0,)), ((), ())), preferred_element_type=F32)


def _mod_kernel(c_ref, w_ref, b_ref, o_ref):
    s = _silu(c_ref[...]).astype(BF)
    o_ref[0] = _dot(s, w_ref[0].astype(BF)) + b_ref[0]


def _modulation(cvec, w_ada, b_ada):
    depth = w_ada.shape[0]
    n = w_ada.shape[2]
    tn = 1536
    return pl.pallas_call(
        _mod_kernel,
        out_shape=jax.ShapeDtypeStruct((depth, MOD_ROWS, n), F32),
        grid=(depth, n // tn),
        in_specs=[
            pl.BlockSpec((MOD_ROWS, D_MODEL), lambda l, j: (0, 0)),
            pl.BlockSpec((1, D_MODEL, tn), lambda l, j: (l, 0, j)),
            pl.BlockSpec((1, 1, tn), lambda l, j: (l, 0, j)),
        ],
        out_specs=pl.BlockSpec((1, MOD_ROWS, tn), lambda l, j: (l, 0, j)),
        compiler_params=_params("parallel", "parallel"),
        name="adaln_mod",
    )(cvec, w_ada, b_ada.reshape(depth, 1, n))


class _Mod:
    def __init__(self, mod_flat, layer, tokens_per_row):
        self.arr = mod_flat
        self.layer = layer
        self.tokens_per_row = tokens_per_row

    def _block(self, row, part):
        return ((self.layer * MOD_ROWS + row) * MOD_PARTS + part, 0, 0)

    def spec(self, part, tile, skip=0):
        def index(i, *_):
            t = jnp.maximum(i - skip, 0) if skip else i
            return self._block(0 if self.tokens_per_row is None else 1 + (t * tile) // self.tokens_per_row, part)

        return pl.BlockSpec((1, 1, D_MODEL), index)

    def spec_batch(self, part):
        def index(b, *_):
            return self._block(0 if self.tokens_per_row is None else 1 + b, part)

        return pl.BlockSpec((1, 1, D_MODEL), index)


def _row_spec(tile, width):
    return pl.BlockSpec((tile, width), lambda i: (i, 0))


def _vec_spec(width):
    return pl.BlockSpec((1, width), lambda *_: (0, 0))


def _positions(n):
    return lax.broadcasted_iota(jnp.int32, (n, 1), 0).astype(F32)


XI_F, XI_B, ZETA_F, ZETA_B = range(4)


def _decay_kernel(lg_ref, dmask_ref, vec_ref, dec_ref):
    hd = pl.program_id(0)
    lgf = lg_ref[0, hd]
    lgb = lg_ref[1, hd]
    c_len = TOKEN_TILE
    qscale = DK_RET ** -0.5
    diff = _positions(c_len) - lax.broadcasted_iota(jnp.int32, (1, c_len), 1).astype(F32)
    dmask_ref[0] = qscale * (jnp.where(diff >= 0, jnp.exp(jnp.maximum(diff, 0.0) * lgf), 0.0)
                             + jnp.where(diff <= 0, jnp.exp(jnp.maximum(-diff, 0.0) * lgb), 0.0))
    n = lax.broadcasted_iota(jnp.int32, (c_len, DK_RET), 0).astype(F32)
    vec_ref[0, XI_F] = qscale * jnp.exp((n + 1.0) * lgf)
    vec_ref[0, XI_B] = qscale * jnp.exp((c_len - n) * lgb)
    vec_ref[0, ZETA_F] = jnp.exp((c_len - 1.0 - n) * lgf)
    vec_ref[0, ZETA_B] = jnp.exp(n * lgb)
    chunk = jnp.full((1, DV_RET), c_len, F32)
    dec_ref[0, 0] = jnp.exp(chunk * lgf)
    dec_ref[0, 1] = jnp.exp(chunk * lgb)


def _decay_tables(lg):
    c_len = TOKEN_TILE
    return pl.pallas_call(
        _decay_kernel,
        out_shape=[jax.ShapeDtypeStruct((H_RET, c_len, c_len), F32),
                   jax.ShapeDtypeStruct((H_RET, 4, c_len, DK_RET), F32),
                   jax.ShapeDtypeStruct((H_RET, 2, 1, DV_RET), F32)],
        grid_spec=pltpu.PrefetchScalarGridSpec(
            num_scalar_prefetch=1, grid=(H_RET,), in_specs=[],
            out_specs=[pl.BlockSpec((1, c_len, c_len), lambda h, lg: (h, 0, 0)),
                       pl.BlockSpec((1, 4, c_len, DK_RET), lambda h, lg: (h, 0, 0, 0)),
                       pl.BlockSpec((1, 2, 1, DV_RET), lambda h, lg: (h, 0, 0, 0))]),
        compiler_params=_params("arbitrary"),
        name="decay_tables",
    )(lg)


def _in_proj_kernel(x_ref, gain_ref, sc_ref, sh_ref, w_ref, vec_ref, dec_ref, *rest, nt, has_s0, emit_state):
    rest = list(rest)
    s0_ref = rest.pop(0) if has_s0 else None
    proj_ref = rest.pop(0)
    sstart_ref = rest.pop(0)
    sfin_ref = rest.pop(0) if emit_state else None
    (s_ref,) = rest
    j = pl.program_id(1)

    @pl.when(j == 0)
    def _():
        s_ref[...] = s0_ref[0, 0, 0] if has_s0 else jnp.zeros_like(s_ref)

    h = _modnorm(x_ref[0], gain_ref[...], sc_ref[0], sh_ref[0]).astype(BF)
    proj = _dot(h, w_ref[...])
    proj_ref[0] = proj.astype(BF)

    for hd in range(H_RET):
        sstart_ref[0, 0, hd] = s_ref[hd].astype(BF)
        k = proj[:, K_OFF + hd * DK_RET:K_OFF + (hd + 1) * DK_RET]
        v = proj[:, V_OFF + hd * DV_RET:V_OFF + (hd + 1) * DV_RET]
        s_ref[hd] = s_ref[hd] * dec_ref[hd, 0] + _dot_tn((k * vec_ref[hd, ZETA_F]).astype(BF), v.astype(BF))

    if emit_state:
        @pl.when(j == nt - 1)
        def _():
            sfin_ref[0] = s_ref[...]


def _in_proj(x3, gain, mod, w, tables, state):
    b, t, _ = x3.shape
    tm = TOKEN_TILE
    nt = t // tm
    has_s0 = state is not None
    emit_state = not has_s0
    _, vec, dec = tables
    in_specs = [pl.BlockSpec((1, tm, D_MODEL), lambda i, j: (i, j, 0)), _vec_spec(D_MODEL),
                mod.spec_batch(1), mod.spec_batch(0), _resident(w.shape), _resident(vec.shape), _resident(dec.shape)]
    args = [x3, gain, mod.arr, mod.arr, w, vec, dec]
    if has_s0:
        in_specs.append(pl.BlockSpec((1, 1, 1, H_RET, DK_RET, DV_RET), lambda i, j: (i, 0, 0, 0, 0, 0)))
        args.append(state)
    out_shape = [jax.ShapeDtypeStruct((b, t, EVEN_IN), BF),
                 jax.ShapeDtypeStruct((b, nt, H_RET, DK_RET, DV_RET), BF)]
    out_specs = [pl.BlockSpec((1, tm, EVEN_IN), lambda i, j: (i, j, 0)),
                 pl.BlockSpec((1, 1, H_RET, DK_RET, DV_RET), lambda i, j: (i, j, 0, 0, 0))]
    if emit_state:
        out_shape.append(jax.ShapeDtypeStruct((b, H_RET, DK_RET, DV_RET), F32))
        out_specs.append(pl.BlockSpec((1, H_RET, DK_RET, DV_RET), lambda i, j: (i, 0, 0, 0)))
    return pl.pallas_call(
        functools.partial(_in_proj_kernel, nt=nt, has_s0=has_s0, emit_state=emit_state),
        out_shape=out_shape,
        grid=(b, nt),
        in_specs=in_specs,
        out_specs=out_specs,
        scratch_shapes=[pltpu.VMEM((H_RET, DK_RET, DV_RET), F32)],
        compiler_params=_params("parallel", "arbitrary"),
        name="in_proj",
    )(*args)


def _mixer_kernel(x_ref, proj_ref, prev_ref, next_ref, sstart_ref, dmask_ref, vec_ref, dec_ref, gn_ref, pw_ref,
                  ps_ref, wout_ref, gain_ref, gate_ref, *rest, nt, t_len, has_s0, emit_state):
    rest = list(rest)
    s0_ref = rest.pop(0) if has_s0 else None
    sfwd_ref = rest.pop(0) if emit_state else None
    o_ref = rest.pop(0)
    sfin_ref = rest.pop(0) if emit_state else None
    s_ref, pad_ref = rest
    j = pl.program_id(1)
    tile = nt - 1 - j
    c_len = TOKEN_TILE

    @pl.when(j == 0)
    def _():
        s_ref[...] = s0_ref[0, 0, 0] if has_s0 else jnp.zeros_like(s_ref)

    gn = gn_ref[...]
    pieces = []
    for hd in range(H_RET):
        q = proj_ref[0, :, Q_OFF + hd * DK_RET:Q_OFF + (hd + 1) * DK_RET]
        k = proj_ref[0, :, K_OFF + hd * DK_RET:K_OFF + (hd + 1) * DK_RET]
        v = proj_ref[0, :, V_OFF + hd * DV_RET:V_OFF + (hd + 1) * DV_RET]
        g = proj_ref[0, :, G_OFF + hd * DV_RET:G_OFF + (hd + 1) * DV_RET].astype(F32)
        scores = _dot_nt(q, k) * dmask_ref[hd]
        inner = _dot(scores.astype(BF), v)
        q32 = q.astype(F32)
        q_both = jnp.concatenate([(q32 * vec_ref[hd, XI_F]).astype(BF), (q32 * vec_ref[hd, XI_B]).astype(BF)],
                                 axis=1)
        s_both = jnp.concatenate([sstart_ref[0, 0, hd], s_ref[hd].astype(BF)], axis=0)
        y = inner + _dot(q_both, s_both)
        s_ref[hd] = s_ref[hd] * dec_ref[hd, 1] + _dot_tn((k.astype(F32) * vec_ref[hd, ZETA_B]).astype(BF), v)

        mu = jnp.mean(y, axis=-1, keepdims=True)
        yc = y - mu
        var = jnp.mean(yc * yc, axis=-1, keepdims=True)
        yn = (yc * lax.rsqrt(var + EPS)) * gn[:, hd * DV_RET:(hd + 1) * DV_RET]
        pieces.append((_silu(g) * yn).astype(BF))

    if emit_state:
        @pl.when(j == nt - 1)
        def _():
            sfin_ref[0, 0] = sfwd_ref[0]
            sfin_ref[0, 1] = s_ref[...]

    prev_rows = prev_ref[0].astype(F32)[HALO_BLOCK - POOL_PAD:]
    next_rows = next_ref[0].astype(F32)[:POOL_PAD]
    pad_ref[0:POOL_PAD, :] = jnp.where(tile > 0, prev_rows, 0.0)
    pad_ref[POOL_PAD:POOL_PAD + c_len, :] = proj_ref[0, :, U_OFF:U_OFF + POOL_WIDTH].astype(F32)
    pad_ref[POOL_PAD + c_len:2 * POOL_PAD + c_len, :] = jnp.where(tile < nt - 1, next_rows, 0.0)
    pos = tile * c_len + lax.broadcasted_iota(jnp.int32, (c_len, 1), 0)
    for grp, window in enumerate(POOL_WINDOWS):
        half = window // 2
        lanes = slice(grp * POOL_GROUP_DIM, (grp + 1) * POOL_GROUP_DIM)
        acc = pad_ref[POOL_PAD - half:POOL_PAD - half + c_len, lanes]
        for off in range(-half + 1, half):
            acc = acc + pad_ref[POOL_PAD + off:POOL_PAD + off + c_len, lanes]
        cnt = (jnp.minimum(pos + half, t_len) - jnp.maximum(pos - half, 0)).astype(F32)
        d = acc / cnt - pad_ref[POOL_PAD:POOL_PAD + c_len, lanes]
        pieces.append((_dot(d.astype(BF), pw_ref[grp]) * ps_ref[:, lanes]).astype(BF))

    o = _dot(jnp.concatenate(pieces, axis=1), wout_ref[...])
    o_ref[0] = x_ref[0] + gate_ref[0] * (_rms(o) * gain_ref[...])


def _mixer(x3, proj, sstart, mod, tables, gn_gain, pool_w, pool_scale, w_out, gain, state, s_fwd):
    b, t, _ = x3.shape
    tm = TOKEN_TILE
    nt = t // tm
    halo_per_tile = tm // HALO_BLOCK
    n_halo = t // HALO_BLOCK
    has_s0 = state is not None
    emit_state = not has_s0
    rev = lambda j: nt - 1 - j
    in_specs = [
        pl.BlockSpec((1, tm, D_MODEL), lambda i, j: (i, rev(j), 0)),
        pl.BlockSpec((1, tm, EVEN_IN), lambda i, j: (i, rev(j), 0)),
        pl.BlockSpec((1, HALO_BLOCK, POOL_WIDTH),
                     lambda i, j: (i, jnp.maximum(rev(j) * halo_per_tile - 1, 0), U_OFF // POOL_WIDTH)),
        pl.BlockSpec((1, HALO_BLOCK, POOL_WIDTH),
                     lambda i, j: (i, jnp.minimum((rev(j) + 1) * halo_per_tile, n_halo - 1), U_OFF // POOL_WIDTH)),
        pl.BlockSpec((1, 1, H_RET, DK_RET, DV_RET), lambda i, j: (i, rev(j), 0, 0, 0)),
        *[_resident(a.shape) for a in tables],
        _vec_spec(H_RET * DV_RET), _resident(pool_w.shape), _vec_spec(POOL_WIDTH), _resident(w_out.shape),
        _vec_spec(D_MODEL), mod.spec_batch(2),
    ]
    args = [x3, proj, proj, proj, sstart, *tables, gn_gain, pool_w, pool_scale, w_out, gain, mod.arr]
    out_shape = [jax.ShapeDtypeStruct((b, t, D_MODEL), F32)]
    out_specs = [pl.BlockSpec((1, tm, D_MODEL), lambda i, j: (i, rev(j), 0))]
    if has_s0:
        in_specs.append(pl.BlockSpec((1, 1, 1, H_RET, DK_RET, DV_RET), lambda i, j: (i, 0, 1, 0, 0, 0)))
        args.append(state)
    else:
        in_specs.append(pl.BlockSpec((1, H_RET, DK_RET, DV_RET), lambda i, j: (i, 0, 0, 0)))
        args.append(s_fwd)
        out_shape.append(jax.ShapeDtypeStruct((b, 2, H_RET, DK_RET, DV_RET), F32))
        out_specs.append(pl.BlockSpec((1, 2, H_RET, DK_RET, DV_RET), lambda i, j: (i, 0, 0, 0, 0)))
    outs = pl.pallas_call(
        functools.partial(_mixer_kernel, nt=nt, t_len=t, has_s0=has_s0, emit_state=emit_state),
        out_shape=out_shape,
        grid=(b, nt),
        in_specs=in_specs,
        out_specs=out_specs,
        scratch_shapes=[pltpu.VMEM((H_RET, DK_RET, DV_RET), F32),
                        pltpu.VMEM((tm + 2 * POOL_PAD, POOL_WIDTH), F32)],
        compiler_params=_params("parallel", "arbitrary"),
        name="mixer",
    )(*args)
    return outs if emit_state else (outs[0], None)


def _swiglu(h, win_ref, wout_ref):
    acts = []
    for c in range(N_FF):
        a = _dot(h, win_ref[c])
        b = _dot(h, win_ref[N_FF + c])
        acts.append((_silu(a) * b).astype(BF))
    return _dot(jnp.concatenate(acts, axis=1), wout_ref[...].reshape(D_FF, D_MODEL))


def _ffn_kernel(xp_ref, xs_ref, gain_in_ref, scp_ref, shp_ref, gatep_ref, scs_ref, shs_ref, gates_ref, gain_out_ref,
                win_ref, wout_ref, op_ref, os_ref, win_bf, wout_bf):
    i = pl.program_id(0)

    @pl.when(i < N_FF)
    def _():
        w = win_ref[0]
        win_bf[2 * i] = w[:, :FF_W].astype(BF)
        win_bf[2 * i + 1] = w[:, FF_W:].astype(BF)
        wout_bf[i] = wout_ref[0].astype(BF)

    @pl.when(i >= N_FF)
    def _():
        hp = _modnorm(xp_ref[...], gain_in_ref[...], scp_ref[0], shp_ref[0]).astype(BF)
        hs = _modnorm(xs_ref[...], gain_in_ref[...], scs_ref[0], shs_ref[0]).astype(BF)
        fp = _swiglu(hp, win_bf, wout_bf)
        op_ref[...] = xp_ref[...] + gatep_ref[0] * (_rms(fp) * gain_out_ref[...])
        fs = _swiglu(hs, win_bf, wout_bf)
        os_ref[...] = xs_ref[...] + gates_ref[0] * (_rms(fs) * gain_out_ref[...])


def _ffn(xp, xs, gain_in, gain_out, mod_p, mod_s, w_in, w_out, layer):
    n_tok = xp.shape[0]
    assert xs.shape == xp.shape
    tm = FFN_TILE
    tile = lambda i: (jnp.maximum(i - N_FF, 0), 0)
    chunk = lambda i: jnp.minimum(i, N_FF - 1)
    mods = [m.spec(part, tm, skip=N_FF) for m in (mod_p, mod_s) for part in (4, 3, 5)]
    return pl.pallas_call(
        _ffn_kernel,
        out_shape=[jax.ShapeDtypeStruct((n_tok, D_MODEL), F32)] * 2,
        grid=(N_FF + n_tok // tm,),
        in_specs=[pl.BlockSpec((tm, D_MODEL), tile), pl.BlockSpec((tm, D_MODEL), tile), _vec_spec(D_MODEL), *mods,
                  _vec_spec(D_MODEL),
                  pl.BlockSpec((1, D_MODEL, 2 * FF_W), lambda i: (layer, 0, chunk(i))),
                  pl.BlockSpec((1, FF_W, D_MODEL), lambda i: (layer, chunk(i), 0))],
        out_specs=[pl.BlockSpec((tm, D_MODEL), tile)] * 2,
        scratch_shapes=[pltpu.VMEM((2 * N_FF, D_MODEL, FF_W), BF), pltpu.VMEM((N_FF, FF_W, D_MODEL), BF)],
        compiler_params=_params("arbitrary"),
        name="ffn",
    )(xp, xs, gain_in, *([mod_p.arr] * 6), gain_out, w_in, w_out)


def _rope(x, cs, sn):
    lane = lax.broadcasted_iota(jnp.int32, x.shape, 1)
    first = (lane % (2 * ROPE_PAIRS)) < ROPE_PAIRS
    partner = jnp.where(first, pltpu.roll(x, HEAD_DIM - ROPE_PAIRS, 1), pltpu.roll(x, ROPE_PAIRS, 1))
    return x * cs + partner * sn


Q_PRESCALE = HEAD_DIM ** -0.5 * 1.4426950408889634


def _qkv_kernel(x_ref, gain_ref, sc_ref, sh_ref, w_ref, qg_ref, kg_ref, *rest, rope, emit_cache):
    rest = list(rest)
    cs_ref = rest.pop(0) if rope else None
    sn_ref = rest.pop(0) if rope else None
    q_ref, kb_ref, vb_ref = rest[:3]
    kf_ref, vf_ref = rest[3:] if emit_cache else (None, None)
    h = _modnorm(x_ref[...], gain_ref[...], sc_ref[0], sh_ref[0]).astype(BF)
    qkv = _dot(h, w_ref[...])

    def head(idx, g_ref):
        xh = _rms(qkv[:, idx * HEAD_DIM:(idx + 1) * HEAD_DIM]) * g_ref[...]
        return _rope(xh, cs_ref[...], sn_ref[...]) if rope else xh

    for i in range(N_HEADS):
        q_ref[:, i * HEAD_DIM:(i + 1) * HEAD_DIM] = (head(i, qg_ref) * Q_PRESCALE).astype(BF)
    for i in range(KV_HEADS):
        lanes = slice(i * HEAD_DIM, (i + 1) * HEAD_DIM)
        kh = head(N_HEADS + i, kg_ref)
        kb_ref[:, lanes] = kh.astype(BF)
        if emit_cache:
            kf_ref[:, lanes] = kh
    v = qkv[:, (N_HEADS + KV_HEADS) * HEAD_DIM:]
    vb_ref[...] = v.astype(BF)
    if emit_cache:
        vf_ref[...] = v


def _qkv(x, gain, mod, w, q_gain, k_gain, rope_tables, emit_cache):
    n_tok = x.shape[0]
    tm = TOKEN_TILE
    rope = rope_tables is not None
    in_specs = [_row_spec(tm, D_MODEL), _vec_spec(D_MODEL), mod.spec(1, tm), mod.spec(0, tm), _resident(w.shape),
                _vec_spec(HEAD_DIM), _vec_spec(HEAD_DIM)]
    args = [x, gain, mod.arr, mod.arr, w, q_gain, k_gain]
    if rope:
        tiles_per_seq = rope_tables[0].shape[0] // tm
        in_specs += [pl.BlockSpec((tm, HEAD_DIM), lambda i: (i % tiles_per_seq, 0))] * 2
        args += list(rope_tables)
    kv_w = KV_HEADS * HEAD_DIM
    widths = [(D_MODEL, BF), (kv_w, BF), (kv_w, BF)] + ([(kv_w, F32), (kv_w, F32)] if emit_cache else [])
    return pl.pallas_call(
        functools.partial(_qkv_kernel, rope=rope, emit_cache=emit_cache),
        out_shape=[jax.ShapeDtypeStruct((n_tok, wd), dt) for wd, dt in widths],
        grid=(n_tok // tm,),
        in_specs=in_specs,
        out_specs=[_row_spec(tm, wd) for wd, _ in widths],
        compiler_params=_params("parallel"),
        name="qkv_proj",
    )(*args)


def _rope_tables(t):
    pos = jnp.arange(t, dtype=jnp.int32)
    row = (pos // GRID_W).astype(F32)
    col = (pos % GRID_W).astype(F32)
    freqs = ROPE_BASE ** (-jnp.arange(ROPE_PAIRS, dtype=F32) / ROPE_PAIRS)
    ang_r = row[:, None] * freqs[None, :]
    ang_c = col[:, None] * freqs[None, :]
    cs = jnp.concatenate([jnp.cos(ang_r)] * 2 + [jnp.cos(ang_c)] * 2, axis=-1)
    sn = jnp.concatenate([-jnp.sin(ang_r), jnp.sin(ang_r), -jnp.sin(ang_c), jnp.sin(ang_c)], axis=-1)
    return cs, sn


KV_BLOCK = 512
LANES = 128


def _fold_lanes(x, op):
    out = x[:, :LANES]
    for i in range(1, x.shape[1] // LANES):
        out = op(out, x[:, i * LANES:(i + 1) * LANES])
    return out


def _attn_kernel(x_ref, q_ref, *rest, seg_lens, blk):
    n_seg = len(seg_lens)
    kv_refs = rest[:2 * n_seg]
    w_ref, gain_ref, gate_ref, o_ref, s_ref, m_ref, acc_ref = rest[2 * n_seg:]
    tq = q_ref.shape[1]
    kv_lanes = [slice(kvh * HEAD_DIM, (kvh + 1) * HEAD_DIM) for kvh in range(KV_HEADS)]

    def over_blocks(body):
        base = 0
        for seg, n in enumerate(seg_lens):
            nb = n // blk
            if nb == 1:
                body(seg, 0, base)
            else:
                def step(i, carry, seg=seg, base=base):
                    body(seg, pl.multiple_of(i * blk, blk), base + i)
                    return carry

                lax.fori_loop(0, nb, step, 0, unroll=True)
            base += nb

    m_ref[...] = jnp.full(m_ref.shape, -jnp.inf, F32)

    def scores(seg, row0, b):
        for kvh in range(KV_HEADS):
            q4 = jnp.concatenate([q_ref[0, :, (kvh * GQA + g) * HEAD_DIM:(kvh * GQA + g + 1) * HEAD_DIM]
                                  for g in range(GQA)], axis=0)
            k = kv_refs[2 * seg][0, pl.ds(row0, blk), kv_lanes[kvh]].astype(BF)
            s = _dot_nt(q4, k)
            s_ref[kvh, b] = s
            m_ref[kvh] = jnp.maximum(m_ref[kvh], _fold_lanes(s, jnp.maximum))

    over_blocks(scores)
    for kvh in range(KV_HEADS):
        m_ref[kvh] = jnp.broadcast_to(jnp.max(m_ref[kvh], axis=-1, keepdims=True), m_ref.shape[1:])

    acc_ref[...] = jnp.zeros(acc_ref.shape, F32)
    ones = jnp.ones((blk, LANES), BF)

    def values(seg, row0, b):
        for kvh in range(KV_HEADS):
            v = kv_refs[2 * seg + 1][0, pl.ds(row0, blk), kv_lanes[kvh]].astype(BF)
            row_max = m_ref[kvh]
            e = jnp.concatenate(
                [jnp.exp2(s_ref[kvh, b, :, t * LANES:(t + 1) * LANES] - row_max).astype(BF)
                 for t in range(blk // LANES)], axis=1)
            acc_ref[kvh] += _dot(e, jnp.concatenate([v, ones], axis=1))

    over_blocks(values)
    pieces = []
    for kvh in range(KV_HEADS):
        att = acc_ref[kvh, :, :HEAD_DIM] / acc_ref[kvh, :, HEAD_DIM:]
        pieces += [att[g * tq:(g + 1) * tq].astype(BF) for g in range(GQA)]

    o = _dot(jnp.concatenate(pieces, axis=1), w_ref[...])
    o_ref[0] = x_ref[0] + gate_ref[0] * (_rms(o) * gain_ref[...])


def _attention(x3, q3, kv_segments, w_o, gain, mod):
    b, t, _ = x3.shape
    tq = TOKEN_TILE
    seg_lens = tuple(k.shape[1] for k, _ in kv_segments)
    blk = min(KV_BLOCK, *seg_lens)
    assert all(n % blk == 0 for n in seg_lens)
    n_blocks = sum(seg_lens) // blk
    rows = GQA * tq
    kv_w = KV_HEADS * HEAD_DIM
    in_specs = [pl.BlockSpec((1, tq, D_MODEL), lambda i, j: (i, j, 0)),
                pl.BlockSpec((1, tq, D_MODEL), lambda i, j: (i, j, 0))]
    args = [x3, q3]
    for k, v in kv_segments:
        in_specs += [pl.BlockSpec((1, k.shape[1], kv_w), lambda i, j: (i, 0, 0))] * 2
        args += [k, v]
    in_specs += [_resident(w_o.shape), _vec_spec(D_MODEL), mod.spec_batch(2)]
    args += [w_o, gain, mod.arr]
    return pl.pallas_call(
        functools.partial(_attn_kernel, seg_lens=seg_lens, blk=blk),
        out_shape=jax.ShapeDtypeStruct(x3.shape, F32),
        grid=(b, t // tq),
        in_specs=in_specs,
        out_specs=pl.BlockSpec((1, tq, D_MODEL), lambda i, j: (i, j, 0)),
        scratch_shapes=[pltpu.VMEM((KV_HEADS, n_blocks, rows, blk), F32),
                        pltpu.VMEM((KV_HEADS, rows, LANES), F32),
                        pltpu.VMEM((KV_HEADS, rows, HEAD_DIM + LANES), F32)],
        compiler_params=_params("parallel", "parallel"),
        name="attention",
    )(*args)


def _even_layer(x3, mod, norm_gain, w_in, tables, gn_gain, pool_w, pool_scale, w_out, state):
    proj, sstart, *s_fwd = _in_proj(x3, norm_gain[0:1], mod, w_in, tables, state)
    x3, s_fin = _mixer(x3, proj, sstart, mod, tables, gn_gain, pool_w, pool_scale, w_out, norm_gain[1:2], state,
                       s_fwd[0] if s_fwd else None)
    return x3.reshape(-1, D_MODEL), s_fin


def _odd_layer(x, b, mod, norm_gain, w_qkv, q_gain, k_gain, w_o, cache, rope_tables):
    t = x.shape[0] // b
    emit_cache = cache is None
    q, kb, vb, *kv_f32 = _qkv(x, norm_gain[0:1], mod, w_qkv, q_gain, k_gain, rope_tables, emit_cache)
    as3 = lambda a: a.reshape(b, t, -1)
    segments = [(as3(kb), as3(vb))]
    if cache is not None:
        segments.insert(0, tuple(a.reshape(b, -1, KV_HEADS * HEAD_DIM) for a in cache))
    x3 = _attention(as3(x), as3(q), segments, w_o, norm_gain[1:2], mod)
    return x3.reshape(-1, D_MODEL), kv_f32


def kernel(x_prompt, x_sample, state_ret, cache_k, cache_v, c, c_ctx, w_ada, b_ada, norm_gain, w_ffn_in, w_ffn_out,
           w_in_even, ret_decay_logit, ret_gn_gain, pool_w, pool_scale, w_out_even, w_qkv, q_norm_gain, k_norm_gain,
           w_o):
    bp, tp, _ = x_prompt.shape
    bs, ts, _ = x_sample.shape
    depth = w_ada.shape[0]
    assert bs + 1 <= MOD_ROWS and depth == 2

    cvec = jnp.zeros((MOD_ROWS, D_MODEL), F32).at[0].set(c_ctx).at[1:1 + bs].set(c)
    mod_flat = _modulation(cvec, w_ada, b_ada).reshape(depth * MOD_ROWS * MOD_PARTS, 1, D_MODEL)
    bf = lambda w: w.astype(BF)

    mod_p = _Mod(mod_flat, 0, None)
    mod_s = _Mod(mod_flat, 0, ts)
    tables = _decay_tables(jax.nn.log_sigmoid(ret_decay_logit[0].astype(F32)))
    even = functools.partial(_even_layer, norm_gain=norm_gain[0], w_in=bf(w_in_even[0]), tables=tables,
                             gn_gain=ret_gn_gain[0:1], pool_w=bf(pool_w[0]), pool_scale=pool_scale[0:1],
                             w_out=bf(w_out_even[0]))
    xp, s_fin = even(x_prompt, mod_p, state=None)
    xs, _ = even(x_sample, mod_s, state=state_ret)
    xp, xs = _ffn(xp, xs, norm_gain[0, 2:3], norm_gain[0, 3:4], mod_p, mod_s, w_ffn_in, w_ffn_out, 0)

    mod_p = _Mod(mod_flat, 1, None)
    mod_s = _Mod(mod_flat, 1, ts)
    odd = functools.partial(_odd_layer, norm_gain=norm_gain[1], w_qkv=bf(w_qkv[0]), q_gain=q_norm_gain[0:1],
                            k_gain=k_norm_gain[0:1], w_o=bf(w_o[0]))
    xp, (kp, vp) = odd(xp, bp, mod_p, cache=None, rope_tables=None)
    xs, _ = odd(xs, bs, mod_s, cache=(cache_k[:, 0], cache_v[:, 0]), rope_tables=_rope_tables(ts))
    xp, xs = _ffn(xp, xs, norm_gain[1, 2:3], norm_gain[1, 3:4], mod_p, mod_s, w_ffn_in, w_ffn_out, 1)

    new_state_ret = s_fin.reshape(bp, 1, 2, H_RET, DK_RET, DV_RET)
    new_cache_k = kp.reshape(bp, 1, tp, KV_HEADS, HEAD_DIM)
    new_cache_v = vp.reshape(bp, 1, tp, KV_HEADS, HEAD_DIM)
    return (xp.reshape(bp, tp, D_MODEL), xs.reshape(bs, ts, D_MODEL), new_state_ret, new_cache_k, new_cache_v)
```

```python
import functools

import jax
import jax.numpy as jnp
import numpy as np
from jax import lax
from jax.experimental import pallas as pl
from jax.experimental.pallas import tpu as pltpu

D_MODEL = 1024
EPS = 1e-6
GRID_W = 64
H_RET = 4
DK_RET = 128
DV_RET = 256
POOL_WINDOWS = (2, 4, 8, 16)
POOL_GROUP_DIM = 128
POOL_WIDTH = 512
POOL_PAD = 8
HALO_BLOCK = 16
Q_OFF, K_OFF, V_OFF, G_OFF, U_OFF = 0, 512, 1024, 2048, 3072
EVEN_IN = 3584
HEAD_DIM = 128
N_HEADS = 8
KV_HEADS = 2
GQA = N_HEADS // KV_HEADS
ROPE_BASE = 10000.0
ROPE_PAIRS = 32
QKV_OUT = (N_HEADS + 2 * KV_HEADS) * HEAD_DIM
D_FF = 2816
FF_W = 256
N_FF = D_FF // FF_W
MOD_ROWS = 8
MOD_PARTS = 6

BF = jnp.bfloat16
F32 = jnp.float32
VMEM_LIMIT_BYTES = 52 * 1024 * 1024
TOKEN_TILE = 256
FFN_TILE = 512


def _params(*sem):
    return pltpu.CompilerParams(dimension_semantics=sem, vmem_limit_bytes=VMEM_LIMIT_BYTES)


def _resident(shape):
    nd = len(shape)
    return pl.BlockSpec(shape, lambda *_: (0,) * nd, pipeline_mode=pl.Buffered(1))


def _rms(x):
    return x * lax.rsqrt(jnp.mean(x * x, axis=-1, keepdims=True) + EPS)


def _modnorm(x, gain, sc, sh):
    return (_rms(x) * gain) * (1.0 + sc) + sh


def _silu(x):
    return x * jax.nn.sigmoid(x)


def _dot(a, b):
    return jnp.dot(a, b, preferred_element_type=F32)


def _dot_nt(a, b):
    return lax.dot_general(a, b, (((1,), (1,)), ((), ())), preferred_element_type=F32)


def _dot_tn(a, b):
    return lax.dot_general(a, b, (((0,), (0,)), ((), ())), preferred_element_type=F32)


def _mod_kernel(c_ref, w_ref, b_ref, o_ref):
    s = _silu(c_ref[...]).astype(BF)
    o_ref[0] = _dot(s, w_ref[0].astype(BF)) + b_ref[0]


def _modulation(cvec, w_ada, b_ada):
    depth = w_ada.shape[0]
    n = w_ada.shape[2]
    tn = 1536
    return pl.pallas_call(
        _mod_kernel,
        out_shape=jax.ShapeDtypeStruct((depth, MOD_ROWS, n), F32),
        grid=(depth, n // tn),
        in_specs=[
            pl.BlockSpec((MOD_ROWS, D_MODEL), lambda l, j: (0, 0)),
            pl.BlockSpec((1, D_MODEL, tn), lambda l, j: (l, 0, j)),
            pl.BlockSpec((1, 1, tn), lambda l, j: (l, 0, j)),
        ],
        out_specs=pl.BlockSpec((1, MOD_ROWS, tn), lambda l, j: (l, 0, j)),
        compiler_params=_params("parallel", "parallel"),
        name="adaln_mod",
    )(cvec, w_ada, b_ada.reshape(depth, 1, n))


class _Mod:
    def __init__(self, mod_flat, layer, tokens_per_row):
        self.arr = mod_flat
        self.layer = layer
        self.tokens_per_row = tokens_per_row

    def _block(self, row, part):
        return ((self.layer * MOD_ROWS + row) * MOD_PARTS + part, 0, 0)

    def spec(self, part, tile, skip=0):
        def index(i, *_):
            t = jnp.maximum(i - skip, 0) if skip else i
            return self._block(0 if self.tokens_per_row is None else 1 + (t * tile) // self.tokens_per_row, part)

        return pl.BlockSpec((1, 1, D_MODEL), index)

    def spec_fn(self, part, row_of_step):
        return pl.BlockSpec((1, 1, D_MODEL), lambda i, *_: self._block(row_of_step(i), part))

    def spec_batch(self, part):
        def index(b, *_):
            return self._block(0 if self.tokens_per_row is None else 1 + b, part)

        return pl.BlockSpec((1, 1, D_MODEL), index)


def _row_spec(tile, width):
    return pl.BlockSpec((tile, width), lambda i: (i, 0))


def _vec_spec(width):
    return pl.BlockSpec((1, width), lambda *_: (0, 0))


def _positions(n):
    return lax.broadcasted_iota(jnp.int32, (n, 1), 0).astype(F32)


XI_F, XI_B, ZETA_F, ZETA_B = range(4)


def _decay_kernel(lg_ref, dmask_ref, vec_ref, dec_ref):
    hd = pl.program_id(0)
    lgf = lg_ref[0, hd]
    lgb = lg_ref[1, hd]
    c_len = TOKEN_TILE
    qscale = DK_RET ** -0.5
    diff = _positions(c_len) - lax.broadcasted_iota(jnp.int32, (1, c_len), 1).astype(F32)
    dmask_ref[0] = qscale * (jnp.where(diff >= 0, jnp.exp(jnp.maximum(diff, 0.0) * lgf), 0.0)
                             + jnp.where(diff <= 0, jnp.exp(jnp.maximum(-diff, 0.0) * lgb), 0.0))
    n = lax.broadcasted_iota(jnp.int32, (c_len, DK_RET), 0).astype(F32)
    vec_ref[0, XI_F] = qscale * jnp.exp((n + 1.0) * lgf)
    vec_ref[0, XI_B] = qscale * jnp.exp((c_len - n) * lgb)
    vec_ref[0, ZETA_F] = jnp.exp((c_len - 1.0 - n) * lgf)
    vec_ref[0, ZETA_B] = jnp.exp(n * lgb)
    chunk = jnp.full((1, DV_RET), c_len, F32)
    dec_ref[0, 0] = jnp.exp(chunk * lgf)
    dec_ref[0, 1] = jnp.exp(chunk * lgb)


def _decay_tables(lg):
    c_len = TOKEN_TILE
    return pl.pallas_call(
        _decay_kernel,
        out_shape=[jax.ShapeDtypeStruct((H_RET, c_len, c_len), F32),
                   jax.ShapeDtypeStruct((H_RET, 4, c_len, DK_RET), F32),
                   jax.ShapeDtypeStruct((H_RET, 2, 1, DV_RET), F32)],
        grid_spec=pltpu.PrefetchScalarGridSpec(
            num_scalar_prefetch=1, grid=(H_RET,), in_specs=[],
            out_specs=[pl.BlockSpec((1, c_len, c_len), lambda h, lg: (h, 0, 0)),
                       pl.BlockSpec((1, 4, c_len, DK_RET), lambda h, lg: (h, 0, 0, 0)),
                       pl.BlockSpec((1, 2, 1, DV_RET), lambda h, lg: (h, 0, 0, 0))]),
        compiler_params=_params("arbitrary"),
        name="decay_tables",
    )(lg)


IN_W = 512
N_IN = EVEN_IN // IN_W


def _in_proj_kernel(xp_ref, xs_ref, gain_ref, scp_ref, shp_ref, scs_ref, shs_ref, w_ref, vec_ref, dec_ref, s0_ref,
                    projp_ref, projs_ref, sstart_ref, sfwd_ref, w_bf, s_ref, *, nt):
    i = pl.program_id(0)
    tm = TOKEN_TILE

    @pl.when(i < N_IN)
    def _():
        w_bf[i] = w_ref[0].astype(BF)

    @pl.when(i >= N_IN)
    def _():
        @pl.when((i - N_IN) % nt == 0)
        def _():
            s_ref[...] = s0_ref[0, 0, 0]

        hp = _modnorm(xp_ref[0], gain_ref[...], scp_ref[0], shp_ref[0]).astype(BF)
        hs = _modnorm(xs_ref[0], gain_ref[...], scs_ref[0], shs_ref[0]).astype(BF)
        h = jnp.concatenate([hp, hs], axis=0)
        kept = {}
        for c in range(N_IN):
            cols = slice(c * IN_W, (c + 1) * IN_W)
            pc = _dot(h, w_bf[c])
            projp_ref[0, :, cols] = pc[:tm].astype(BF)
            projs_ref[0, :, cols] = pc[tm:].astype(BF)
            if K_OFF <= c * IN_W < G_OFF:
                kept[c] = pc

        def columns(off, width):
            return kept[off // IN_W][:, off % IN_W:off % IN_W + width]

        for hd in range(H_RET):
            k = columns(K_OFF + hd * DK_RET, DK_RET)
            v = columns(V_OFF + hd * DV_RET, DV_RET).astype(BF)
            zeta = vec_ref[hd, ZETA_F]
            sfwd_ref[0, hd] = _dot_tn((k[:tm] * zeta).astype(BF), v[:tm])
            sstart_ref[0, 0, hd] = s_ref[hd].astype(BF)
            s_ref[hd] = s_ref[hd] * dec_ref[hd, 0] + _dot_tn((k[tm:] * zeta).astype(BF), v[tm:])


def _in_proj(xp3, xs3, gain, mod_p, mod_s, w_in, layer, tables, state):
    bp, tp, _ = xp3.shape
    bs, ts, _ = xs3.shape
    tm = TOKEN_TILE
    nt = ts // tm
    assert tp == tm and bp == bs * nt
    _, vec, dec = tables
    step = lambda i: jnp.maximum(i - N_IN, 0)
    seq = lambda i: (step(i), 0, 0)
    lat = lambda i: (step(i) // nt, step(i) % nt, 0)
    return pl.pallas_call(
        functools.partial(_in_proj_kernel, nt=nt),
        out_shape=[jax.ShapeDtypeStruct((bp, tp, EVEN_IN), BF), jax.ShapeDtypeStruct((bs, ts, EVEN_IN), BF),
                   jax.ShapeDtypeStruct((bs, nt, H_RET, DK_RET, DV_RET), BF),
                   jax.ShapeDtypeStruct((bp, H_RET, DK_RET, DV_RET), F32)],
        grid=(N_IN + bp,),
        in_specs=[pl.BlockSpec((1, tm, D_MODEL), seq), pl.BlockSpec((1, tm, D_MODEL), lat), _vec_spec(D_MODEL),
                  mod_p.spec_fn(1, lambda i: 0), mod_p.spec_fn(0, lambda i: 0),
                  mod_s.spec_fn(1, lambda i: 1 + step(i) // nt), mod_s.spec_fn(0, lambda i: 1 + step(i) // nt),
                  pl.BlockSpec((1, D_MODEL, IN_W), lambda i: (layer, 0, jnp.minimum(i, N_IN - 1))),
                  _resident(vec.shape), _resident(dec.shape),
                  pl.BlockSpec((1, 1, 1, H_RET, DK_RET, DV_RET), lambda i: (step(i) // nt, layer, 0, 0, 0, 0))],
        out_specs=[pl.BlockSpec((1, tm, EVEN_IN), seq), pl.BlockSpec((1, tm, EVEN_IN), lat),
                   pl.BlockSpec((1, 1, H_RET, DK_RET, DV_RET), lambda i: (step(i) // nt, step(i) % nt, 0, 0, 0)),
                   pl.BlockSpec((1, H_RET, DK_RET, DV_RET), lambda i: (step(i), 0, 0, 0))],
        scratch_shapes=[pltpu.VMEM((N_IN, D_MODEL, IN_W), BF), pltpu.VMEM((H_RET, DK_RET, DV_RET), F32)],
        compiler_params=_params("arbitrary"),
        name="in_proj",
    )(xp3, xs3, gain, mod_p.arr, mod_p.arr, mod_s.arr, mod_s.arr, w_in, vec, dec, state)


def _mix_tile(x_ref, proj_ref, dmask_ref, vec_ref, gn_ref, pw_ref, ps_ref, wout_ref, gain_ref, gate_ref, o_ref, pad_ref,
              halo, pos0, t_len, carry):
    c_len = TOKEN_TILE
    gn = gn_ref[...]
    pieces = []
    for hd in range(H_RET):
        q = proj_ref[0, :, Q_OFF + hd * DK_RET:Q_OFF + (hd + 1) * DK_RET]
        k = proj_ref[0, :, K_OFF + hd * DK_RET:K_OFF + (hd + 1) * DK_RET]
        v = proj_ref[0, :, V_OFF + hd * DV_RET:V_OFF + (hd + 1) * DV_RET]
        g = proj_ref[0, :, G_OFF + hd * DV_RET:G_OFF + (hd + 1) * DV_RET].astype(F32)
        scores = _dot_nt(q, k) * dmask_ref[hd]
        kz = (k.astype(F32) * vec_ref[hd, ZETA_B]).astype(BF)
        y = carry(hd, q, kz, v, _dot(scores.astype(BF), v))

        mu = jnp.mean(y, axis=-1, keepdims=True)
        yc = y - mu
        var = jnp.mean(yc * yc, axis=-1, keepdims=True)
        yn = (yc * lax.rsqrt(var + EPS)) * gn[:, hd * DV_RET:(hd + 1) * DV_RET]
        pieces.append((_silu(g) * yn).astype(BF))

    zeros = jnp.zeros((POOL_PAD, POOL_WIDTH), F32)
    pad_ref[0:POOL_PAD, :] = zeros if halo is None else halo[0]
    pad_ref[POOL_PAD:POOL_PAD + c_len, :] = proj_ref[0, :, U_OFF:U_OFF + POOL_WIDTH].astype(F32)
    pad_ref[POOL_PAD + c_len:2 * POOL_PAD + c_len, :] = zeros if halo is None else halo[1]
    pos = pos0 + lax.broadcasted_iota(jnp.int32, (c_len, 1), 0)
    for grp, window in enumerate(POOL_WINDOWS):
        half = window // 2
        lanes = slice(grp * POOL_GROUP_DIM, (grp + 1) * POOL_GROUP_DIM)
        acc = pad_ref[POOL_PAD - half:POOL_PAD - half + c_len, lanes]
        for off in range(-half + 1, half):
            acc = acc + pad_ref[POOL_PAD + off:POOL_PAD + off + c_len, lanes]
        cnt = (jnp.minimum(pos + half, t_len) - jnp.maximum(pos - half, 0)).astype(F32)
        d = acc / cnt - pad_ref[POOL_PAD:POOL_PAD + c_len, lanes]
        pieces.append((_dot(d.astype(BF), pw_ref[grp]) * ps_ref[:, lanes]).astype(BF))

    o = _dot(jnp.concatenate(pieces, axis=1), wout_ref[...])
    o_ref[0] = x_ref[0] + gate_ref[0] * (_rms(o) * gain_ref[...])


def _mixer_kernel(xp_ref, xs_ref, projp_ref, projs_ref, prev_ref, next_ref, sstart_ref, sfwd_ref, dmask_ref, vec_ref,
                  dec_ref, gn_ref, pw_ref, ps_ref, wout_ref, gain_ref, gatep_ref, gates_ref, s0_ref,
                  op_ref, os_ref, sfin_ref, s_ref, padp_ref, pads_ref, *, nt, t_len):
    j = pl.program_id(0) % nt
    tile = nt - 1 - j
    c_len = TOKEN_TILE
    shared = (dmask_ref, vec_ref, gn_ref, pw_ref, ps_ref, wout_ref, gain_ref)

    def context_carry(hd, q, kz, v, y):
        sfin_ref[0, 1, hd] = _dot_tn(kz, v)
        return y

    sfin_ref[0, 0] = sfwd_ref[0]
    _mix_tile(xp_ref, projp_ref, *shared, gatep_ref, op_ref, padp_ref, None, 0, c_len, context_carry)

    @pl.when(j == 0)
    def _():
        s_ref[...] = s0_ref[0, 0, 0]

    def latent_carry(hd, q, kz, v, y):
        q32 = q.astype(F32)
        q_both = jnp.concatenate([(q32 * vec_ref[hd, XI_F]).astype(BF), (q32 * vec_ref[hd, XI_B]).astype(BF)],
                                 axis=1)
        s_both = jnp.concatenate([sstart_ref[0, 0, hd], s_ref[hd].astype(BF)], axis=0)
        s_ref[hd] = s_ref[hd] * dec_ref[hd, 1] + _dot_tn(kz, v)
        return y + _dot(q_both, s_both)

    halo = (jnp.where(tile > 0, prev_ref[0].astype(F32)[HALO_BLOCK - POOL_PAD:], 0.0),
            jnp.where(tile < nt - 1, next_ref[0].astype(F32)[:POOL_PAD], 0.0))
    _mix_tile(xs_ref, projs_ref, *shared, gates_ref, os_ref, pads_ref, halo, tile * c_len, t_len, latent_carry)


def _mixer(xp3, xs3, proj_p, proj_s, sstart, s_fwd, mod_p, mod_s, tables, gn_gain, pool_w, pool_scale, w_out, gain,
           layer, state):
    bp, tp, _ = xp3.shape
    bs, ts, _ = xs3.shape
    tm = TOKEN_TILE
    nt = ts // tm
    assert tp == tm and bp == bs * nt
    halo_per_tile = tm // HALO_BLOCK
    n_halo = ts // HALO_BLOCK
    u_blk = U_OFF // POOL_WIDTH
    tile = lambda i: nt - 1 - i % nt
    seq = lambda i: (i, 0, 0)
    lat = lambda i: (i // nt, tile(i), 0)
    state_blk = (1, H_RET, DK_RET, DV_RET)
    return pl.pallas_call(
        functools.partial(_mixer_kernel, nt=nt, t_len=ts),
        out_shape=[jax.ShapeDtypeStruct(xp3.shape, F32), jax.ShapeDtypeStruct(xs3.shape, F32),
                   jax.ShapeDtypeStruct((bp, 2, H_RET, DK_RET, DV_RET), F32)],
        grid=(bp,),
        in_specs=[
            pl.BlockSpec((1, tm, D_MODEL), seq), pl.BlockSpec((1, tm, D_MODEL), lat),
            pl.BlockSpec((1, tm, EVEN_IN), seq), pl.BlockSpec((1, tm, EVEN_IN), lat),
            pl.BlockSpec((1, HALO_BLOCK, POOL_WIDTH),
                         lambda i: (i // nt, jnp.maximum(tile(i) * halo_per_tile - 1, 0), u_blk)),
            pl.BlockSpec((1, HALO_BLOCK, POOL_WIDTH),
                         lambda i: (i // nt, jnp.minimum((tile(i) + 1) * halo_per_tile, n_halo - 1), u_blk)),
            pl.BlockSpec((1,) + state_blk, lambda i: (i // nt, tile(i), 0, 0, 0)),
            pl.BlockSpec(state_blk, lambda i: (i, 0, 0, 0)),
            *[_resident(a.shape) for a in tables],
            _vec_spec(H_RET * DV_RET), _resident(pool_w.shape), _vec_spec(POOL_WIDTH), _resident(w_out.shape),
            _vec_spec(D_MODEL), mod_p.spec_fn(2, lambda i: 0), mod_s.spec_fn(2, lambda i: 1 + i // nt),
            pl.BlockSpec((1, 1, 1) + state_blk[1:], lambda i: (i // nt, layer, 1, 0, 0, 0)),
        ],
        out_specs=[pl.BlockSpec((1, tm, D_MODEL), seq), pl.BlockSpec((1, tm, D_MODEL), lat),
                   pl.BlockSpec((1, 2) + state_blk[1:], lambda i: (i, 0, 0, 0, 0))],
        scratch_shapes=[pltpu.VMEM((H_RET, DK_RET, DV_RET), F32),
                        pltpu.VMEM((tm + 2 * POOL_PAD, POOL_WIDTH), F32),
                        pltpu.VMEM((tm + 2 * POOL_PAD, POOL_WIDTH), F32)],
        compiler_params=_params("arbitrary"),
        name="mixer",
    )(xp3, xs3, proj_p, proj_s, proj_s, proj_s, sstart, s_fwd, *tables, gn_gain, pool_w, pool_scale, w_out, gain,
      mod_p.arr, mod_s.arr, state)


def _swiglu(h, win_ref, wout_ref):
    acts = []
    for c in range(N_FF):
        a = _dot(h, win_ref[c])
        b = _dot(h, win_ref[N_FF + c])
        acts.append((_silu(a) * b).astype(BF))
    return _dot(jnp.concatenate(acts, axis=1), wout_ref[...].reshape(D_FF, D_MODEL))


def _ffn_kernel(xp_ref, xs_ref, gain_in_ref, scp_ref, shp_ref, gatep_ref, scs_ref, shs_ref, gates_ref, gain_out_ref,
                win_ref, wout_ref, op_ref, os_ref, win_bf, wout_bf):
    i = pl.program_id(0)

    @pl.when(i < N_FF)
    def _():
        w = win_ref[0]
        win_bf[2 * i] = w[:, :FF_W].astype(BF)
        win_bf[2 * i + 1] = w[:, FF_W:].astype(BF)
        wout_bf[i] = wout_ref[0].astype(BF)

    @pl.when(i >= N_FF)
    def _():
        hp = _modnorm(xp_ref[...], gain_in_ref[...], scp_ref[0], shp_ref[0]).astype(BF)
        hs = _modnorm(xs_ref[...], gain_in_ref[...], scs_ref[0], shs_ref[0]).astype(BF)
        fp = _swiglu(hp, win_bf, wout_bf)
        op_ref[...] = xp_ref[...] + gatep_ref[0] * (_rms(fp) * gain_out_ref[...])
        fs = _swiglu(hs, win_bf, wout_bf)
        os_ref[...] = xs_ref[...] + gates_ref[0] * (_rms(fs) * gain_out_ref[...])


def _ffn(xp, xs, gain_in, gain_out, mod_p, mod_s, w_in, w_out, layer):
    n_tok = xp.shape[0]
    assert xs.shape == xp.shape
    tm = FFN_TILE
    tile = lambda i: (jnp.maximum(i - N_FF, 0), 0)
    chunk = lambda i: jnp.minimum(i, N_FF - 1)
    mods = [m.spec(part, tm, skip=N_FF) for m in (mod_p, mod_s) for part in (4, 3, 5)]
    return pl.pallas_call(
        _ffn_kernel,
        out_shape=[jax.ShapeDtypeStruct((n_tok, D_MODEL), F32)] * 2,
        grid=(N_FF + n_tok // tm,),
        in_specs=[pl.BlockSpec((tm, D_MODEL), tile), pl.BlockSpec((tm, D_MODEL), tile), _vec_spec(D_MODEL), *mods,
                  _vec_spec(D_MODEL),
                  pl.BlockSpec((1, D_MODEL, 2 * FF_W), lambda i: (layer, 0, chunk(i))),
                  pl.BlockSpec((1, FF_W, D_MODEL), lambda i: (layer, chunk(i), 0))],
        out_specs=[pl.BlockSpec((tm, D_MODEL), tile)] * 2,
        scratch_shapes=[pltpu.VMEM((2 * N_FF, D_MODEL, FF_W), BF), pltpu.VMEM((N_FF, FF_W, D_MODEL), BF)],
        compiler_params=_params("arbitrary"),
        name="ffn",
    )(xp, xs, gain_in, *([mod_p.arr] * 6), gain_out, w_in, w_out)


def _rope(x, cs, sn):
    lane = lax.broadcasted_iota(jnp.int32, x.shape, 1)
    first = (lane % (2 * ROPE_PAIRS)) < ROPE_PAIRS
    partner = jnp.where(first, pltpu.roll(x, HEAD_DIM - ROPE_PAIRS, 1), pltpu.roll(x, ROPE_PAIRS, 1))
    return x * cs + partner * sn


Q_PRESCALE = HEAD_DIM ** -0.5 * 1.4426950408889634


def _qkv_kernel(x_ref, gain_ref, sc_ref, sh_ref, w_ref, qg_ref, kg_ref, *rest, rope, emit_cache):
    rest = list(rest)
    cs_ref = rest.pop(0) if rope else None
    sn_ref = rest.pop(0) if rope else None
    q_ref, kb_ref, vb_ref = rest[:3]
    kf_ref, vf_ref = rest[3:] if emit_cache else (None, None)
    h = _modnorm(x_ref[...], gain_ref[...], sc_ref[0], sh_ref[0]).astype(BF)
    qkv = _dot(h, w_ref[...])

    def head(idx, g_ref):
        xh = _rms(qkv[:, idx * HEAD_DIM:(idx + 1) * HEAD_DIM]) * g_ref[...]
        return _rope(xh, cs_ref[...], sn_ref[...]) if rope else xh

    for i in range(N_HEADS):
        q_ref[:, i * HEAD_DIM:(i + 1) * HEAD_DIM] = (head(i, qg_ref) * Q_PRESCALE).astype(BF)
    for i in range(KV_HEADS):
        lanes = slice(i * HEAD_DIM, (i + 1) * HEAD_DIM)
        kh = head(N_HEADS + i, kg_ref)
        kb_ref[:, lanes] = kh.astype(BF)
        if emit_cache:
            kf_ref[:, lanes] = kh
    v = qkv[:, (N_HEADS + KV_HEADS) * HEAD_DIM:]
    vb_ref[...] = v.astype(BF)
    if emit_cache:
        vf_ref[...] = v


def _qkv(x, gain, mod, w, q_gain, k_gain, rope_tables, emit_cache):
    n_tok = x.shape[0]
    tm = TOKEN_TILE
    rope = rope_tables is not None
    in_specs = [_row_spec(tm, D_MODEL), _vec_spec(D_MODEL), mod.spec(1, tm), mod.spec(0, tm), _resident(w.shape),
                _vec_spec(HEAD_DIM), _vec_spec(HEAD_DIM)]
    args = [x, gain, mod.arr, mod.arr, w, q_gain, k_gain]
    if rope:
        tiles_per_seq = rope_tables[0].shape[0] // tm
        in_specs += [pl.BlockSpec((tm, HEAD_DIM), lambda i: (i % tiles_per_seq, 0))] * 2
        args += list(rope_tables)
    kv_w = KV_HEADS * HEAD_DIM
    widths = [(D_MODEL, BF), (kv_w, BF), (kv_w, BF)] + ([(kv_w, F32), (kv_w, F32)] if emit_cache else [])
    return pl.pallas_call(
        functools.partial(_qkv_kernel, rope=rope, emit_cache=emit_cache),
        out_shape=[jax.ShapeDtypeStruct((n_tok, wd), dt) for wd, dt in widths],
        grid=(n_tok // tm,),
        in_specs=in_specs,
        out_specs=[_row_spec(tm, wd) for wd, _ in widths],
        compiler_params=_params("parallel"),
        name="qkv_proj",
    )(*args)


def _rope_tables(t):
    pos = np.arange(t)
    freqs = (np.float32(ROPE_BASE) ** (-np.arange(ROPE_PAIRS, dtype=np.float32) / ROPE_PAIRS)).astype(np.float32)
    ang_r = ((pos // GRID_W).astype(np.float32)[:, None] * freqs[None, :]).astype(np.float64)
    ang_c = ((pos % GRID_W).astype(np.float32)[:, None] * freqs[None, :]).astype(np.float64)
    cs = np.concatenate([np.cos(ang_r)] * 2 + [np.cos(ang_c)] * 2, axis=-1)
    sn = np.concatenate([-np.sin(ang_r), np.sin(ang_r), -np.sin(ang_c), np.sin(ang_c)], axis=-1)
    return jnp.asarray(cs, F32), jnp.asarray(sn, F32)


KV_BLOCK = 512
LANES = 128


def _fold_lanes(x, op):
    out = x[:, :LANES]
    for i in range(1, x.shape[1] // LANES):
        out = op(out, x[:, i * LANES:(i + 1) * LANES])
    return out


def _attn_kernel(x_ref, q_ref, *rest, seg_lens, blk):
    n_seg = len(seg_lens)
    kv_refs = rest[:2 * n_seg]
    w_ref, gain_ref, gate_ref, o_ref, s_ref, m_ref, acc_ref = rest[2 * n_seg:]
    tq = q_ref.shape[1]
    kv_lanes = [slice(kvh * HEAD_DIM, (kvh + 1) * HEAD_DIM) for kvh in range(KV_HEADS)]

    def over_blocks(body):
        base = 0
        for seg, n in enumerate(seg_lens):
            nb = n // blk
            if nb == 1:
                body(seg, 0, base)
            else:
                def step(i, carry, seg=seg, base=base):
                    body(seg, pl.multiple_of(i * blk, blk), base + i)
                    return carry

                lax.fori_loop(0, nb, step, 0, unroll=True)
            base += nb

    m_ref[...] = jnp.full(m_ref.shape, -jnp.inf, F32)

    def scores(seg, row0, b):
        for kvh in range(KV_HEADS):
            q4 = jnp.concatenate([q_ref[0, :, (kvh * GQA + g) * HEAD_DIM:(kvh * GQA + g + 1) * HEAD_DIM]
                                  for g in range(GQA)], axis=0)
            k = kv_refs[2 * seg][0, pl.ds(row0, blk), kv_lanes[kvh]].astype(BF)
            s = _dot_nt(q4, k)
            s_ref[kvh, b] = s
            m_ref[kvh] = jnp.maximum(m_ref[kvh], _fold_lanes(s, jnp.maximum))

    over_blocks(scores)
    for kvh in range(KV_HEADS):
        m_ref[kvh] = jnp.broadcast_to(jnp.max(m_ref[kvh], axis=-1, keepdims=True), m_ref.shape[1:])

    acc_ref[...] = jnp.zeros(acc_ref.shape, F32)
    ones = jnp.ones((blk, LANES), BF)

    def values(seg, row0, b):
        for kvh in range(KV_HEADS):
            v = kv_refs[2 * seg + 1][0, pl.ds(row0, blk), kv_lanes[kvh]].astype(BF)
            row_max = m_ref[kvh]
            e = jnp.concatenate(
                [jnp.exp2(s_ref[kvh, b, :, t * LANES:(t + 1) * LANES] - row_max).astype(BF)
                 for t in range(blk // LANES)], axis=1)
            acc_ref[kvh] += _dot(e, jnp.concatenate([v, ones], axis=1))

    over_blocks(values)
    pieces = []
    for kvh in range(KV_HEADS):
        att = acc_ref[kvh, :, :HEAD_DIM] / acc_ref[kvh, :, HEAD_DIM:]
        pieces += [att[g * tq:(g + 1) * tq].astype(BF) for g in range(GQA)]

    o = _dot(jnp.concatenate(pieces, axis=1), w_ref[...])
    o_ref[0] = x_ref[0] + gate_ref[0] * (_rms(o) * gain_ref[...])


def _attention(x3, q3, kv_segments, w_o, gain, mod):
    b, t, _ = x3.shape
    tq = TOKEN_TILE
    seg_lens = tuple(k.shape[1] for k, _ in kv_segments)
    blk = min(KV_BLOCK, *seg_lens)
    assert all(n % blk == 0 for n in seg_lens)
    n_blocks = sum(seg_lens) // blk
    rows = GQA * tq
    kv_w = KV_HEADS * HEAD_DIM
    in_specs = [pl.BlockSpec((1, tq, D_MODEL), lambda i, j: (i, j, 0)),
                pl.BlockSpec((1, tq, D_MODEL), lambda i, j: (i, j, 0))]
    args = [x3, q3]
    for k, v in kv_segments:
        in_specs += [pl.BlockSpec((1, k.shape[1], kv_w), lambda i, j: (i, 0, 0))] * 2
        args += [k, v]
    in_specs += [_resident(w_o.shape), _vec_spec(D_MODEL), mod.spec_batch(2)]
    args += [w_o, gain, mod.arr]
    return pl.pallas_call(
        functools.partial(_attn_kernel, seg_lens=seg_lens, blk=blk),
        out_shape=jax.ShapeDtypeStruct(x3.shape, F32),
        grid=(b, t // tq),
        in_specs=in_specs,
        out_specs=pl.BlockSpec((1, tq, D_MODEL), lambda i, j: (i, j, 0)),
        scratch_shapes=[pltpu.VMEM((KV_HEADS, n_blocks, rows, blk), F32),
                        pltpu.VMEM((KV_HEADS, rows, LANES), F32),
                        pltpu.VMEM((KV_HEADS, rows, HEAD_DIM + LANES), F32)],
        compiler_params=_params("parallel", "parallel"),
        name="attention",
    )(*args)


def _odd_layer(x, b, mod, norm_gain, w_qkv, q_gain, k_gain, w_o, cache, rope_tables):
    t = x.shape[0] // b
    emit_cache = cache is None
    q, kb, vb, *kv_f32 = _qkv(x, norm_gain[0:1], mod, w_qkv, q_gain, k_gain, rope_tables, emit_cache)
    as3 = lambda a: a.reshape(b, t, -1)
    segments = [(as3(kb), as3(vb))]
    if cache is not None:
        segments.insert(0, tuple(a.reshape(b, -1, KV_HEADS * HEAD_DIM) for a in cache))
    x3 = _attention(as3(x), as3(q), segments, w_o, norm_gain[1:2], mod)
    return x3.reshape(-1, D_MODEL), kv_f32


def kernel(x_prompt, x_sample, state_ret, cache_k, cache_v, c, c_ctx, w_ada, b_ada, norm_gain, w_ffn_in, w_ffn_out,
           w_in_even, ret_decay_logit, ret_gn_gain, pool_w, pool_scale, w_out_even, w_qkv, q_norm_gain, k_norm_gain,
           w_o):
    bp, tp, _ = x_prompt.shape
    bs, ts, _ = x_sample.shape
    depth = w_ada.shape[0]
    assert bs + 1 <= MOD_ROWS and depth == 2

    cvec = jnp.zeros((MOD_ROWS, D_MODEL), F32).at[0].set(c_ctx).at[1:1 + bs].set(c)
    mod_flat = _modulation(cvec, w_ada, b_ada).reshape(depth * MOD_ROWS * MOD_PARTS, 1, D_MODEL)
    bf = lambda w: w.astype(BF)

    mod_p = _Mod(mod_flat, 0, None)
    mod_s = _Mod(mod_flat, 0, ts)
    tables = _decay_tables(jax.nn.log_sigmoid(ret_decay_logit[0].astype(F32)))
    proj_p, proj_s, sstart, s_fwd = _in_proj(x_prompt, x_sample, norm_gain[0, 0:1], mod_p, mod_s, w_in_even, 0,
                                             tables, state_ret)
    xp, xs, s_fin = _mixer(x_prompt, x_sample, proj_p, proj_s, sstart, s_fwd, mod_p, mod_s, tables,
                           ret_gn_gain[0:1], bf(pool_w[0]), pool_scale[0:1], bf(w_out_even[0]), norm_gain[0, 1:2],
                           0, state_ret)
    xp = xp.reshape(-1, D_MODEL)
    xs = xs.reshape(-1, D_MODEL)
    xp, xs = _ffn(xp, xs, norm_gain[0, 2:3], norm_gain[0, 3:4], mod_p, mod_s, w_ffn_in, w_ffn_out, 0)

    mod_p = _Mod(mod_flat, 1, None)
    mod_s = _Mod(mod_flat, 1, ts)
    odd = functools.partial(_odd_layer, norm_gain=norm_gain[1], w_qkv=bf(w_qkv[0]), q_gain=q_norm_gain[0:1],
                            k_gain=k_norm_gain[0:1], w_o=bf(w_o[0]))
    xp, (kp, vp) = odd(xp, bp, mod_p, cache=None, rope_tables=None)
    xs, _ = odd(xs, bs, mod_s, cache=(cache_k[:, 0], cache_v[:, 0]), rope_tables=_rope_tables(ts))
    xp, xs = _ffn(xp, xs, norm_gain[1, 2:3], norm_gain[1, 3:4], mod_p, mod_s, w_ffn_in, w_ffn_out, 1)

    new_state_ret = s_fin.reshape(bp, 1, 2, H_RET, DK_RET, DV_RET)
    new_cache_k = kp.reshape(bp, 1, tp, KV_HEADS, HEAD_DIM)
    new_cache_v = vp.reshape(bp, 1, tp, KV_HEADS, HEAD_DIM)
    return (xp.reshape(bp, tp, D_MODEL), xs.reshape(bs, ts, D_MODEL), new_state_ret, new_cache_k, new_cache_v)
```
